```python
import jax, jax.numpy as jnp
from jax import lax
import numpy as np

D_MODEL = 2048
BATCH = 8
SEQ = 4096
DEPTH = 4

GRID_W = 64
CTX_LEN = 256
N_MIXERS = 4
MIX_GMLP, MIX_CONV, MIX_ATTN, MIX_FNET = 0, 1, 2, 3
CHUNK = 128
GMLP_WIDTH = D_MODEL
GMLP_GROUPS = 8
CONV_WIDTH = 31
CONV_PAD = CONV_WIDTH // 2
HEAD_DIM = 128
N_HEADS = D_MODEL // HEAD_DIM
N_KV_HEADS = N_HEADS // 4
GQA_GROUP = N_HEADS // N_KV_HEADS
WINDOW = 128
ATTN_BLOCK = 128
ROPE_BASE = 10000.0
FNET_GROUPS = 8
D_FF = 5632
N_EXPERTS = 8
TOP_K = 2
D_FF_EXPERT = 2 * D_MODEL
MOE_BLOCK = 512
EPS = 1e-6

kernel_name = "hybrid_interleaved_diffusion_block"

F32 = jnp.float32


def _n_layers_of(kind):
    return len([i for i in range(DEPTH) if i % N_MIXERS == kind])


def rms_norm(x, g):
    xf = x.astype(F32)
    y = xf * lax.rsqrt(jnp.mean(xf * xf, axis=-1, keepdims=True) + EPS)
    return (y * g.astype(F32)).astype(x.dtype)


def _modulate(t, shift, scale):
    return t * (1 + scale) + shift


def swiglu(x, w_gate, w_up, w_down):
    return (jax.nn.silu(x @ w_gate) * (x @ w_up)) @ w_down


def gmlp_mix(h, w_in, v_g, w_s, b_s, w_out):
    B, L, _ = h.shape
    u, v = jnp.split(jax.nn.gelu(h @ w_in), 2, axis=-1)
    v = rms_norm(v, v_g)
    vc = v.reshape(B, L // CHUNK, CHUNK, GMLP_GROUPS, GMLP_WIDTH // GMLP_GROUPS)
    sv = jnp.einsum('gpq,bnqgc->bnpgc', w_s, vc) + b_s.T[None, None, :, :, None]
    return (u * sv.reshape(B, L, GMLP_WIDTH)) @ w_out


def conformer_conv(h, w_pw1, b_pw1, w_dw, b_dw, n_g, w_pw2):
    a, gate = jnp.split(h @ w_pw1 + b_pw1, 2, axis=-1)
    z = a * jax.nn.sigmoid(gate)
    z = lax.conv_general_dilated(
        z, w_dw[:, None, :].astype(z.dtype), window_strides=(1,),
        padding=((CONV_PAD, CONV_PAD),), dimension_numbers=('NWC', 'WIO', 'NWC'),
        feature_group_count=D_MODEL) + b_dw
    z = jax.nn.silu(rms_norm(z, n_g))
    return z @ w_pw2


def axial_angles(L):
    rows = L // GRID_W
    row = jnp.repeat(jnp.arange(rows, dtype=F32), GRID_W)
    col = jnp.tile(jnp.arange(GRID_W, dtype=F32), rows)
    axis_dim = HEAD_DIM // 2
    inv_freq = ROPE_BASE ** (-jnp.arange(0, axis_dim, 2, dtype=F32) / axis_dim)
    return row[:, None, None] * inv_freq, col[:, None, None] * inv_freq


def _rope_half(x, ang):
    x1, x2 = jnp.split(x, 2, axis=-1)
    cos, sin = jnp.cos(ang), jnp.sin(ang)
    return jnp.concatenate([x1 * cos - x2 * sin, x2 * cos + x1 * sin], axis=-1)


def axial_rope(x, ang_r, ang_c):
    xr, xc = jnp.split(x.astype(F32), 2, axis=-1)
    return jnp.concatenate([_rope_half(xr, ang_r), _rope_half(xc, ang_c)], axis=-1).astype(x.dtype)


def _heads(t, n):
    return t.reshape(*t.shape[:-1], n, HEAD_DIM)


def sink_attend(q, k, v, valid, sink):
    s = jnp.einsum('bqkgd,bskd->bkgqs', q, k, preferred_element_type=F32) * (HEAD_DIM ** -0.5)
    if valid is not None:
        s = jnp.where(valid, s, -jnp.inf)
    sk = jnp.broadcast_to(sink.astype(F32).reshape(1, N_KV_HEADS, GQA_GROUP, 1, 1), s.shape[:-1] + (1,))
    p = jax.nn.softmax(jnp.concatenate([s, sk], axis=-1), axis=-1)[..., :-1]
    return jnp.einsum('bkgqs,bskd->bqkgd', p.astype(v.dtype), v)


def windowed_gqa(y, yc, ctx_out, w_qkv, q_g, k_g, sink, w_o, ang_r, ang_c):
    B, L, _ = y.shape
    q_dim, kv_dim = N_HEADS * HEAD_DIM, N_KV_HEADS * HEAD_DIM
    qkv = y @ w_qkv
    q = axial_rope(rms_norm(_heads(qkv[..., :q_dim], N_HEADS), q_g), ang_r, ang_c)
    k = axial_rope(rms_norm(_heads(qkv[..., q_dim:q_dim + kv_dim], N_KV_HEADS), k_g), ang_r, ang_c)
    v = _heads(qkv[..., q_dim + kv_dim:], N_KV_HEADS)
    kv_c = yc @ w_qkv[:, q_dim:]
    kc = rms_norm(_heads(kv_c[..., :kv_dim], N_KV_HEADS), k_g)
    vc = _heads(kv_c[..., kv_dim:], N_KV_HEADS)
    n_ctx = yc.shape[1]

    nblk = L // ATTN_BLOCK
    span = ATTN_BLOCK + 2 * WINDOW
    kp = jnp.pad(k, ((0, 0), (WINDOW, WINDOW), (0, 0), (0, 0)))
    vp = jnp.pad(v, ((0, 0), (WINDOW, WINDOW), (0, 0), (0, 0)))
    qb = q.reshape(B, nblk, ATTN_BLOCK, N_KV_HEADS, GQA_GROUP, HEAD_DIM).transpose(1, 0, 2, 3, 4, 5)
    r_idx = jnp.arange(ATTN_BLOCK)[:, None]
    s_idx = jnp.arange(span)[None, :]
    band = (s_idx - r_idx >= 0) & (s_idx - r_idx <= 2 * WINDOW)
    ctx_valid = jnp.ones((ATTN_BLOCK, n_ctx), dtype=bool)

    def block(args):
        qi, bi = args
        start = bi * ATTN_BLOCK
        kb = lax.dynamic_slice_in_dim(kp, start, span, axis=1)
        vb = lax.dynamic_slice_in_dim(vp, start, span, axis=1)
        key_pos = start - WINDOW + s_idx
        valid = band & (key_pos >= 0) & (key_pos < L)
        valid = jnp.concatenate([valid, ctx_valid], axis=-1)
        return sink_attend(qi, jnp.concatenate([kb, kc], axis=1),
                           jnp.concatenate([vb, vc], axis=1), valid, sink)

    o = lax.map(block, (qb, jnp.arange(nblk)))
    o = o.transpose(1, 0, 2, 3, 4, 5).reshape(B, L, q_dim)
    out = o @ w_o
    out_c = None
    if ctx_out:
        qc = rms_norm(_heads(yc @ w_qkv[:, :q_dim], N_HEADS), q_g)
        qc = qc.reshape(B, n_ctx, N_KV_HEADS, GQA_GROUP, HEAD_DIM)
        out_c = sink_attend(qc, kc, vc, None, sink).reshape(B, n_ctx, q_dim) @ w_o
    return out, out_c


def fourier_mix(h, w_out):
    B, L, _ = h.shape
    hg = h.astype(F32).reshape(B, L, FNET_GROUPS, D_MODEL // FNET_GROUPS)
    f = jnp.fft.fft2(hg, axes=(1, 3), norm='ortho').real
    return f.reshape(B, L, D_MODEL).astype(h.dtype) @ w_out


def moe_swiglu(x, w_router, b_router, w_gate, w_up, w_down):
    N, D = x.shape
    logits = jnp.matmul(x, w_router, preferred_element_type=F32) + b_router.astype(F32)
    top_logit, top_idx = lax.top_k(logits, TOP_K)
    top_w = jax.nn.softmax(top_logit, axis=-1)
    flat_e = top_idx.reshape(-1)
    flat_tok = jnp.repeat(jnp.arange(N), TOP_K)
    flat_w = top_w.reshape(-1)
    order = jnp.argsort(flat_e)
    se, stok, sw = flat_e[order], flat_tok[order], flat_w[order]
    counts = jnp.bincount(flat_e, length=N_EXPERTS)
    padded = (counts + MOE_BLOCK - 1) // MOE_BLOCK * MOE_BLOCK
    start = jnp.cumsum(counts) - counts
    pstart = jnp.cumsum(padded) - padded
    dest = pstart[se] + jnp.arange(N * TOP_K) - start[se]
    n_rows = (-(-(N * TOP_K) // MOE_BLOCK) + N_EXPERTS) * MOE_BLOCK
    row_tok = jnp.full((n_rows,), N, dtype=jnp.int32).at[dest].set(stok)
    row_w = jnp.zeros((n_rows,), F32).at[dest].set(sw)
    nb = n_rows // MOE_BLOCK
    block_e = jnp.minimum(jnp.searchsorted(jnp.cumsum(padded), jnp.arange(nb) * MOE_BLOCK, side='right'),
                          N_EXPERTS - 1)
    x_pad = jnp.concatenate([x, jnp.zeros((1, D), x.dtype)], axis=0)
    xb = x_pad[row_tok].reshape(nb, MOE_BLOCK, D)

    def expert_block(args):
        xi, e = args
        return swiglu(xi, w_gate[e], w_up[e], w_down[e])

    yb = lax.map(expert_block, (xb, block_e)).reshape(n_rows, D) * row_w[:, None].astype(x.dtype)
    return jnp.zeros((N + 1, D), x.dtype).at[row_tok].add(yb)[:N]


def setup_inputs(seed: int = 0) -> dict:
    key = jax.random.key(seed)
    ks = iter(jax.random.split(key, 48))

    def nrm(shape, scale):
        return jax.random.normal(next(ks), shape, F32) * scale

    def gain(shape):
        return 1.0 + nrm(shape, 0.02)

    D = D_MODEL
    n_gm, n_cv, n_at, n_ft = (_n_layers_of(k) for k in range(N_MIXERS))
    n_dense, n_moe = (DEPTH + 1) // 2, DEPTH // 2
    qkv_dim = (N_HEADS + 2 * N_KV_HEADS) * HEAD_DIM
    return {
        "x": nrm((BATCH, SEQ, D), 1.0),
        "c": nrm((BATCH, D), 1.0),
        "ctx": nrm((BATCH, CTX_LEN, D), 1.0),
        "c_ctx": nrm((D,), 1.0),
        "ada_w": nrm((DEPTH, D, 6 * D), 0.5 * D ** -0.5),
        "ada_b": nrm((DEPTH, 6 * D), 0.02),
        "norm_mix_g": gain((DEPTH, D)),
        "norm_ffn_g": gain((DEPTH, D)),
        "gm_w_in": nrm((n_gm, D, 2 * GMLP_WIDTH), D ** -0.5),
        "gm_v_g": gain((n_gm, GMLP_WIDTH)),
        "gm_w_s": nrm((n_gm, GMLP_GROUPS, CHUNK, CHUNK), CHUNK ** -0.5),
        "gm_b_s": 1.0 + nrm((n_gm, GMLP_GROUPS, CHUNK), 0.02),
        "gm_w_out": nrm((n_gm, GMLP_WIDTH, D), GMLP_WIDTH ** -0.5),
        "cv_w_pw1": nrm((n_cv, D, 2 * D), D ** -0.5),
        "cv_b_pw1": nrm((n_cv, 2 * D), 0.02),
        "cv_w_dw": nrm((n_cv, CONV_WIDTH, D), CONV_WIDTH ** -0.5),
        "cv_b_dw": nrm((n_cv, D), 0.02),
        "cv_norm_g": gain((n_cv, D)),
        "cv_w_pw2": nrm((n_cv, D, D), D ** -0.5),
        "at_w_qkv": nrm((n_at, D, qkv_dim), D ** -0.5),
        "at_q_g": gain((n_at, HEAD_DIM)),
        "at_k_g": gain((n_at, HEAD_DIM)),
        "at_sink": nrm((n_at, N_HEADS), 0.5),
        "at_w_o": nrm((n_at, N_HEADS * HEAD_DIM, D), (N_HEADS * HEAD_DIM) ** -0.5),
        "ft_w_out": nrm((n_ft, D, D), D ** -0.5),
        "f_w_gate": nrm((n_dense, D, D_FF), D ** -0.5),
        "f_w_up": nrm((n_dense, D, D_FF), D ** -0.5),
        "f_w_down": nrm((n_dense, D_FF, D), D_FF ** -0.5),
        "m_w_router": nrm((n_moe, D, N_EXPERTS), D ** -0.5),
        "m_b_router": nrm((n_moe, N_EXPERTS), 0.01),
        "m_w_gate": nrm((n_moe, N_EXPERTS, D, D_FF_EXPERT), D ** -0.5),
        "m_w_up": nrm((n_moe, N_EXPERTS, D, D_FF_EXPERT), D ** -0.5),
        "m_w_down": nrm((n_moe, N_EXPERTS, D_FF_EXPERT, D), D_FF_EXPERT ** -0.5),
    }


def reference(x, c, ctx, c_ctx, ada_w, ada_b, norm_mix_g, norm_ffn_g,
              gm_w_in, gm_v_g, gm_w_s, gm_b_s, gm_w_out,
              cv_w_pw1, cv_b_pw1, cv_w_dw, cv_b_dw, cv_norm_g, cv_w_pw2,
              at_w_qkv, at_q_g, at_k_g, at_sink, at_w_o,
              ft_w_out,
              f_w_gate, f_w_up, f_w_down,
              m_w_router, m_b_router, m_w_gate, m_w_up, m_w_down):
    B, L, D = x.shape
    ang_r, ang_c = axial_angles(L)
    silu_c = jax.nn.silu(c)
    silu_cc = jax.nn.silu(c_ctx)
    last_ctx_reader = max([i for i in range(DEPTH) if i % N_MIXERS == MIX_ATTN], default=-1)

    for i in range(DEPTH):
        kind, j = i % N_MIXERS, i // N_MIXERS
        ctx_in = i <= last_ctx_reader
        ctx_out = i < last_ctx_reader

        shift1, scale1, gate1, shift2, scale2, gate2 = jnp.split(
            (silu_c @ ada_w[i] + ada_b[i])[:, None, :], 6, axis=-1)
        y = _modulate(rms_norm(x, norm_mix_g[i]), shift1, scale1)
        yc = None
        if ctx_in:
            cshift1, cscale1, cgate1, cshift2, cscale2, cgate2 = jnp.split(
                silu_cc @ ada_w[i] + ada_b[i], 6)
            yc = _modulate(rms_norm(ctx, norm_mix_g[i]), cshift1, cscale1)

        mix_c = None
        if kind == MIX_ATTN:
            mix, mix_c = windowed_gqa(y, yc, ctx_out, at_w_qkv[j], at_q_g[j], at_k_g[j],
                                      at_sink[j], at_w_o[j], ang_r, ang_c)
        else:
            if kind == MIX_GMLP:
                mixer = lambda t: gmlp_mix(t, gm_w_in[j], gm_v_g[j], gm_w_s[j], gm_b_s[j], gm_w_out[j])
            elif kind == MIX_CONV:
                mixer = lambda t: conformer_conv(t, cv_w_pw1[j], cv_b_pw1[j], cv_w_dw[j], cv_b_dw[j],
                                                 cv_norm_g[j], cv_w_pw2[j])
            else:
                mixer = lambda t: fourier_mix(t, ft_w_out[j])
            mix = mixer(y)
            if ctx_out:
                mix_c = mixer(yc)
        x = x + gate1 * mix
        if ctx_out:
            ctx = ctx + cgate1 * mix_c

        y = _modulate(rms_norm(x, norm_ffn_g[i]), shift2, scale2)
        if ctx_out:
            yc = _modulate(rms_norm(ctx, norm_ffn_g[i]), cshift2, cscale2)
            y = jnp.concatenate([yc, y], axis=1)
        tokens = y.reshape(-1, D)
        if i % 2 == 0:
            out = swiglu(tokens, f_w_gate[i // 2], f_w_up[i // 2], f_w_down[i // 2])
        else:
            out = moe_swiglu(tokens, m_w_router[i // 2], m_b_router[i // 2],
                             m_w_gate[i // 2], m_w_up[i // 2], m_w_down[i // 2])
        out = out.reshape(B, -1, D)
        if ctx_out:
            n_ctx = ctx.shape[1]
            ctx = ctx + cgate2 * out[:, :n_ctx]
            out = out[:, n_ctx:]
        x = x + gate2 * out
    return x
```

```python
import functools
import math

import numpy as np
import jax
import jax.numpy as jnp
from jax import lax
from jax.experimental import pallas as pl
from jax.experimental.pallas import tpu as pltpu

F32 = jnp.float32
BF16 = jnp.bfloat16

EPS = 1e-6
GRID_W = 64
CHUNK = 128
GMLP_GROUPS = 8
CONV_WIDTH = 31
CONV_PAD = CONV_WIDTH // 2
HEAD_DIM = 128
KV_GROUP = 4
WINDOW = 128
ROPE_BASE = 10000.0
FNET_GROUPS = 8
TOP_K = 2

LANE = 128
V7X_VMEM_LIMIT = 56 << 20

ROW_TILE = 1024
COL_TILE = 512
CONV_HALO = 16
MOE_TILE = 1024
DMA_ROWS = 512


def _params(sem, vmem_bytes):
    return pltpu.CompilerParams(dimension_semantics=sem,
                                vmem_limit_bytes=int(min(max(vmem_bytes, 16 << 20), V7X_VMEM_LIMIT)))


def _pow2_tile(target, *sizes):
    t = target
    while any(s % t for s in sizes):
        t //= 2
    return t


def _sigmoid(x):
    return 1.0 / (1.0 + jnp.exp(-x))


def _norm_mod(x, g, shift, scale):
    ms = jnp.mean(x * x, axis=-1, keepdims=True)
    return (x * lax.rsqrt(ms + EPS) * g) * (1.0 + scale) + shift


def _norm_mod_rows(x_ref, g_ref, sh_ref, sc_ref, dst_ref, dtype, chunk=256):
    rows = x_ref.shape[0]
    chunk = min(chunk, rows)
    g, sh, sc = g_ref[...], sh_ref[0], sc_ref[0]

    def body(r, carry):
        sl = pl.ds(pl.multiple_of(r * chunk, chunk), chunk)
        dst_ref[sl, :] = _norm_mod(x_ref[sl, :], g, sh, sc).astype(dtype)
        return carry

    lax.fori_loop(0, rows // chunk, body, 0)


def _ada_kernel(c_ref, w_ref, b_ref, o_ref):
    c = c_ref[...]
    s = (c * _sigmoid(c)).astype(BF16)
    o_ref[0] = jnp.dot(s, w_ref[0].astype(BF16), preferred_element_type=F32) + b_ref[0]


def _ada_table(cc, ada_w, ada_b):
    depth, d, n6 = ada_w.shape
    tn = _pow2_tile(1024, n6)
    rows = cc.shape[0]
    return pl.pallas_call(
        _ada_kernel,
        grid=(depth, n6 // tn),
        in_specs=[pl.BlockSpec((rows, d), lambda l, j: (0, 0)),
                  pl.BlockSpec((1, d, tn), lambda l, j: (l, 0, j)),
                  pl.BlockSpec((1, 1, tn), lambda l, j: (l, 0, j))],
        out_specs=pl.BlockSpec((1, rows, tn), lambda l, j: (l, 0, j)),
        out_shape=jax.ShapeDtypeStruct((depth, rows, n6), F32),
        compiler_params=_params(("arbitrary", "arbitrary"), 2 * d * tn * 4 + d * tn * 2 + (8 << 20)),
        name="ada_table",
    )(cc, ada_w, ada_b.reshape(depth, 1, n6))


class _Mod:
    def __init__(self, table, tm, seq_len, n_batch):
        self.table, self.tm, self.seq_len, self.n_batch = table, tm, seq_len, n_batch
        self.d = table.shape[-1]

    def spec(self, chunk):
        tm, seq_len, n_batch = self.tm, self.seq_len, self.n_batch

        def index(i, *_):
            return (jnp.minimum(i * tm // seq_len, n_batch) * 6 + chunk, 0, 0)

        return pl.BlockSpec((1, 1, self.d), index)


def _gelu_tanh(x):
    return 0.5 * x * (1.0 + jnp.tanh(math.sqrt(2.0 / math.pi) * (x + 0.044715 * (x * x * x))))


def _proj_gelu_kernel(x_ref, g_ref, sh_ref, sc_ref, w_ref, o_ref, y_scr):
    @pl.when(pl.program_id(1) == 0)
    def _():
        _norm_mod_rows(x_ref, g_ref, sh_ref, sc_ref, y_scr, BF16)

    acc = jnp.dot(y_scr[...], w_ref[...], preferred_element_type=F32)
    o_ref[...] = _gelu_tanh(acc).astype(o_ref.dtype)


def _proj_glu_kernel(x_ref, g_ref, sh_ref, sc_ref, wa_ref, wg_ref, ba_ref, bg_ref, o_ref, y_scr):
    @pl.when(pl.program_id(1) == 0)
    def _():
        _norm_mod_rows(x_ref, g_ref, sh_ref, sc_ref, y_scr, BF16)

    y = y_scr[...]
    a = jnp.dot(y, wa_ref[...], preferred_element_type=F32) + ba_ref[...]
    gate = jnp.dot(y, wg_ref[...], preferred_element_type=F32) + bg_ref[...]
    o_ref[...] = (a * _sigmoid(gate)).astype(o_ref.dtype)


def _rope_partner(x):
    lane = lax.broadcasted_iota(jnp.int32, x.shape, 1)
    return jnp.where((lane % (HEAD_DIM // 2)) < HEAD_DIM // 4,
                     pltpu.roll(x, HEAD_DIM - HEAD_DIM // 4, 1), pltpu.roll(x, HEAD_DIM // 4, 1))


def _proj_qkv_kernel(n_qk_tiles, x_ref, g_ref, sh_ref, sc_ref, w_ref, hg_ref, cos_ref, sin_ref, o_ref, y_scr):
    j = pl.program_id(1)

    @pl.when(j == 0)
    def _():
        _norm_mod_rows(x_ref, g_ref, sh_ref, sc_ref, y_scr, BF16)

    acc = jnp.dot(y_scr[...], w_ref[...], preferred_element_type=F32)

    @pl.when(j < n_qk_tiles)
    def _():
        hg, cos, sin = hg_ref[0], cos_ref[...], sin_ref[...]
        for h in range(acc.shape[1] // HEAD_DIM):
            t = acc[:, h * HEAD_DIM:(h + 1) * HEAD_DIM]
            t = t * lax.rsqrt(jnp.mean(t * t, axis=-1, keepdims=True) + EPS) * hg
            t = t * cos + _rope_partner(t) * sin
            o_ref[:, h * HEAD_DIM:(h + 1) * HEAD_DIM] = t.astype(o_ref.dtype)

    @pl.when(j >= n_qk_tiles)
    def _():
        o_ref[...] = acc.astype(o_ref.dtype)


def _proj_call(kernel, x, gain, mod, chunk0, weights, extra_in, extra_specs, n_out, tm, tn, name):
    nt, d = x.shape
    in_specs = [pl.BlockSpec((tm, d), lambda i, j: (i, 0)),
                pl.BlockSpec((1, d), lambda i, j: (0, 0)),
                mod.spec(chunk0), mod.spec(chunk0 + 1)]
    args = [x, gain.reshape(1, d), mod.table, mod.table]
    for w, off in weights:
        in_specs.append(pl.BlockSpec((d, tn), functools.partial(lambda i, j, off: (0, j + off), off=off)))
        args.append(w)
    in_specs += extra_specs
    args += extra_in
    vmem = 2 * tm * d * 4 + tm * d * 2 + len(weights) * 2 * d * tn * 2 + 2 * tm * tn * 2 + 6 * tm * tn * 4
    return pl.pallas_call(
        kernel,
        grid=(nt // tm, n_out // tn),
        in_specs=in_specs,
        out_specs=pl.BlockSpec((tm, tn), lambda i, j: (i, j)),
        out_shape=jax.ShapeDtypeStruct((nt, n_out), BF16),
        scratch_shapes=[pltpu.VMEM((tm, d), BF16)],
        compiler_params=_params(("arbitrary", "arbitrary"), vmem + (4 << 20)),
        name=name,
    )(*args)


def _resid_mm_kernel(a_ref, w_ref, x_ref, gt_ref, o_ref):
    o_ref[...] = x_ref[...] + gt_ref[0] * jnp.dot(a_ref[...], w_ref[...], preferred_element_type=F32)


def _resid_mm(a, w, x, mod, gate_chunk, n_rows, tm, name):
    k, d = w.shape
    return pl.pallas_call(
        _resid_mm_kernel,
        grid=(n_rows // tm,),
        in_specs=[pl.BlockSpec((tm, k), lambda i: (i, 0)),
                  pl.BlockSpec((k, d), lambda i: (0, 0)),
                  pl.BlockSpec((tm, d), lambda i: (i, 0)),
                  mod.spec(gate_chunk)],
        out_specs=pl.BlockSpec((tm, d), lambda i: (i, 0)),
        out_shape=jax.ShapeDtypeStruct((n_rows, d), F32),
        compiler_params=_params(("arbitrary",), 2 * tm * k * 2 + 2 * k * d * 2 + 5 * tm * d * 4 + (4 << 20)),
        name=name,
    )(a, w, x, mod.table)


def _gmlp_tail_kernel(h_ref, vg_ref, ws_ref, bs_ref, w_ref, x_ref, gt_ref, o_ref, z_scr):
    tm, d = x_ref.shape
    gw = d // GMLP_GROUPS
    v = h_ref[:, d:].astype(F32)
    vn = (v * lax.rsqrt(jnp.mean(v * v, axis=-1, keepdims=True) + EPS) * vg_ref[...]).astype(BF16)
    for c in range(tm // CHUNK):
        rows = slice(c * CHUNK, (c + 1) * CHUNK)
        for g in range(GMLP_GROUPS):
            cols = slice(g * gw, (g + 1) * gw)
            sv = jnp.dot(ws_ref[g], vn[rows, cols], preferred_element_type=F32) + bs_ref[g]
            z_scr[rows, cols] = (h_ref[rows, cols].astype(F32) * sv).astype(BF16)
    o_ref[...] = x_ref[...] + gt_ref[0] * jnp.dot(z_scr[...], w_ref[...], preferred_element_type=F32)


def _gmlp_tail(h, v_g, w_s, b_s, w_out, x, mod, tm):
    nt, d = x.shape
    bs_tile = jnp.broadcast_to(b_s[:, :, None], (GMLP_GROUPS, CHUNK, d // GMLP_GROUPS)).astype(F32)
    return pl.pallas_call(
        _gmlp_tail_kernel,
        grid=(nt // tm,),
        in_specs=[pl.BlockSpec((tm, 2 * d), lambda i: (i, 0)),
                  pl.BlockSpec((1, d), lambda i: (0, 0)),
                  pl.BlockSpec((GMLP_GROUPS, CHUNK, CHUNK), lambda i: (0, 0, 0)),
                  pl.BlockSpec((GMLP_GROUPS, CHUNK, d // GMLP_GROUPS), lambda i: (0, 0, 0)),
                  pl.BlockSpec((d, d), lambda i: (0, 0)),
                  pl.BlockSpec((tm, d), lambda i: (i, 0)),
                  mod.spec(2)],
        out_specs=pl.BlockSpec((tm, d), lambda i: (i, 0)),
        out_shape=jax.ShapeDtypeStruct((nt, d), F32),
        scratch_shapes=[pltpu.VMEM((tm, d), BF16)],
        compiler_params=_params(("arbitrary",), 2 * tm * 2 * d * 2 + 2 * d * d * 2 + 6 * tm * d * 4 + (6 << 20)),
        name="gmlp_tail",
    )(h, v_g.reshape(1, d), w_s.astype(BF16), bs_tile, w_out, x, mod.table)


def _conv_tail_kernel(edges, z_ref, zp_ref, zn_ref, wdw_ref, bdw_ref, ng_ref, w_ref, x_ref, gt_ref,
                      o_ref, ext_scr, cv_scr):
    i = pl.program_id(0)
    tm, d = x_ref.shape
    first, last = edges(i)
    ext_scr[0:CONV_HALO, :] = jnp.where(first, 0.0, zp_ref[...].astype(F32))
    ext_scr[CONV_HALO:CONV_HALO + tm, :] = z_ref[...].astype(F32)
    ext_scr[CONV_HALO + tm:, :] = jnp.where(last, 0.0, zn_ref[...].astype(F32))

    rc = min(64, tm)
    cc = min(256, d)

    def col_body(c, carry):
        cols = pl.ds(pl.multiple_of(c * cc, cc), cc)
        wts = wdw_ref[:, cols]
        bias = bdw_ref[:, cols]
        for r in range(tm // rc):
            acc = jnp.zeros((rc, cc), F32) + bias
            for k in range(CONV_WIDTH):
                off = r * rc + CONV_HALO - CONV_PAD + k
                acc = acc + wts[k:k + 1, :] * ext_scr[off:off + rc, cols]
            cv_scr[r * rc:(r + 1) * rc, cols] = acc
        return carry

    lax.fori_loop(0, d // cc, col_body, 0)
    cv = cv_scr[...]
    t = cv * lax.rsqrt(jnp.mean(cv * cv, axis=-1, keepdims=True) + EPS) * ng_ref[...]
    t = (t * _sigmoid(t)).astype(BF16)
    o_ref[...] = x_ref[...] + gt_ref[0] * jnp.dot(t, w_ref[...], preferred_element_type=F32)


def _conv_tail(z, w_dw, b_dw, n_g, w_pw2, x, mod, tm, seq_len, n_lat, ctx_len):
    nt, d = x.shape
    hb = tm // CONV_HALO
    n_halo_blocks = nt // CONV_HALO
    lat_blocks, seq_blocks, ctx_blocks = n_lat // tm, seq_len // tm, ctx_len // tm

    def edges(i):
        in_lat = i < lat_blocks
        pos = jnp.where(in_lat, i % seq_blocks, (i - lat_blocks) % ctx_blocks)
        per = jnp.where(in_lat, seq_blocks, ctx_blocks)
        return pos == 0, pos == per - 1

    kdw = w_dw.shape[0]
    kpad = -(-kdw // 8) * 8
    w_dw_p = jnp.zeros((kpad, d), F32).at[:kdw].set(w_dw)
    return pl.pallas_call(
        functools.partial(_conv_tail_kernel, edges),
        grid=(nt // tm,),
        in_specs=[pl.BlockSpec((tm, d), lambda i: (i, 0)),
                  pl.BlockSpec((CONV_HALO, d), lambda i: (jnp.maximum(i * hb - 1, 0), 0)),
                  pl.BlockSpec((CONV_HALO, d), lambda i: (jnp.minimum((i + 1) * hb, n_halo_blocks - 1), 0)),
                  pl.BlockSpec((kpad, d), lambda i: (0, 0)),
                  pl.BlockSpec((1, d), lambda i: (0, 0)),
                  pl.BlockSpec((1, d), lambda i: (0, 0)),
                  pl.BlockSpec((d, d), lambda i: (0, 0)),
                  pl.BlockSpec((tm, d), lambda i: (i, 0)),
                  mod.spec(2)],
        out_specs=pl.BlockSpec((tm, d), lambda i: (i, 0)),
        out_shape=jax.ShapeDtypeStruct((nt, d), F32),
        scratch_shapes=[pltpu.VMEM((tm + 2 * CONV_HALO, d), F32), pltpu.VMEM((tm, d), F32)],
        compiler_params=_params(("arbitrary",), 2 * d * d * 2 + 12 * tm * d * 4 + (6 << 20)),
        name="conv_tail",
    )(z, z, z, w_dw_p, b_dw.reshape(1, d), n_g.reshape(1, d), w_pw2, x, mod.table)


def _copy_rows(src_ref, dst_ref, chunk=256):
    rows = src_ref.shape[0]
    chunk = min(chunk, rows)

    def body(r, carry):
        sl = pl.ds(pl.multiple_of(r * chunk, chunk), chunk)
        dst_ref[sl, :] = src_ref[sl, :].astype(dst_ref.dtype)
        return carry

    lax.fori_loop(0, rows // chunk, body, 0)


def _ffn_kernel(x_ref, g_ref, sh_ref, sc_ref, gt_ref, wg_ref, wu_ref, wd_ref, o_ref, y_scr):
    @pl.when(pl.program_id(1) == 0)
    def _():
        _norm_mod_rows(x_ref, g_ref, sh_ref, sc_ref, y_scr, BF16)
        _copy_rows(x_ref, o_ref)

    y = y_scr[...]
    hg = jnp.dot(y, wg_ref[...], preferred_element_type=F32)
    hu = jnp.dot(y, wu_ref[...], preferred_element_type=F32)
    h = (hg * _sigmoid(hg) * hu).astype(BF16)
    o_ref[...] += gt_ref[0] * jnp.dot(h, wd_ref[...], preferred_element_type=F32)


def _ffn(x, gain, mod, w_gate, w_up, w_down, tm):
    nt, d = x.shape
    dff = w_gate.shape[1]
    tf = 512 if dff % 512 == 0 else 256
    while dff % tf:
        tf //= 2
    vmem = 4 * tm * d * 4 + tm * d * 2 + 6 * d * tf * 2 + 5 * tm * tf * 4
    return pl.pallas_call(
        _ffn_kernel,
        grid=(nt // tm, dff // tf),
        in_specs=[pl.BlockSpec((tm, d), lambda i, f: (i, 0)),
                  pl.BlockSpec((1, d), lambda i, f: (0, 0)),
                  mod.spec(3), mod.spec(4), mod.spec(5),
                  pl.BlockSpec((d, tf), lambda i, f: (0, f)),
                  pl.BlockSpec((d, tf), lambda i, f: (0, f)),
                  pl.BlockSpec((tf, d), lambda i, f: (f, 0))],
        out_specs=pl.BlockSpec((tm, d), lambda i, f: (i, 0)),
        out_shape=jax.ShapeDtypeStruct((nt, d), F32),
        scratch_shapes=[pltpu.VMEM((tm, d), BF16)],
        compiler_params=_params(("arbitrary", "arbitrary"), vmem + (4 << 20)),
        name="ffn_dense",
    )(x, gain.reshape(1, d), mod.table, mod.table, mod.table, w_gate, w_up, w_down)


def _route_kernel(n_experts, x_ref, g_ref, sh_ref, sc_ref, wh_ref, wl_ref, br_ref, tri_ref,
                  y_ref, info_ref, cnt_ref, carry):
    i = pl.program_id(0)

    @pl.when(i == 0)
    def _():
        carry[...] = jnp.zeros_like(carry)

    _norm_mod_rows(x_ref, g_ref, sh_ref, sc_ref, y_ref, F32)
    y = y_ref[...]
    yh = y.astype(BF16)
    yl = (y - yh.astype(F32)).astype(BF16)
    wh, wl = wh_ref[...], wl_ref[...]
    lg = (jnp.dot(yh, wh, preferred_element_type=F32) + jnp.dot(yh, wl, preferred_element_type=F32)
          + jnp.dot(yl, wh, preferred_element_type=F32) + br_ref[...])
    lane = lax.broadcasted_iota(jnp.int32, lg.shape, 1)
    neg = jnp.float32(-jnp.inf)
    lg = jnp.where(lane < n_experts, lg, neg)
    m1 = jnp.max(lg, axis=-1, keepdims=True)
    i1 = jnp.min(jnp.where(lg == m1, lane, LANE), axis=-1, keepdims=True)
    lg2 = jnp.where(lane == i1, neg, lg)
    m2 = jnp.max(lg2, axis=-1, keepdims=True)
    i2 = jnp.min(jnp.where(lg2 == m2, lane, LANE), axis=-1, keepdims=True)
    e = jnp.exp(m2 - m1)
    w1 = 1.0 / (1.0 + e)
    w2 = e * w1
    hit1, hit2 = lane == i1, lane == i2
    onehot = (hit1 | hit2).astype(F32)
    pre = jnp.dot(tri_ref[...], onehot.astype(BF16), preferred_element_type=F32) + carry[0:1, :]
    r1 = jnp.sum(jnp.where(hit1, pre, 0.0), axis=-1, keepdims=True)
    r2 = jnp.sum(jnp.where(hit2, pre, 0.0), axis=-1, keepdims=True)
    total = carry[0:1, :] + jnp.sum(onehot, axis=0, keepdims=True)
    carry[...] = jnp.broadcast_to(total, carry.shape)
    cnt_ref[...] = jnp.broadcast_to(total, cnt_ref.shape)
    info = jnp.where(lane == 0, i1.astype(F32), 0.0)
    info = jnp.where(lane == 1, i2.astype(F32), info)
    info = jnp.where(lane == 2, w1, info)
    info = jnp.where(lane == 3, w2, info)
    info = jnp.where(lane == 4, r1, info)
    info = jnp.where(lane == 5, r2, info)
    info_ref[...] = info


def _route(x, gain, mod, w_router, b_router, tm):
    nt, d = x.shape
    n_experts = w_router.shape[1]
    wr = jnp.zeros((d, LANE), F32).at[:, :n_experts].set(w_router)
    wr_hi = wr.astype(BF16)
    wr_lo = (wr - wr_hi.astype(F32)).astype(BF16)
    br = jnp.zeros((1, LANE), F32).at[0, :n_experts].set(b_router)
    tri = jnp.tril(jnp.ones((tm, tm), BF16), -1)
    return pl.pallas_call(
        functools.partial(_route_kernel, n_experts),
        grid=(nt // tm,),
        in_specs=[pl.BlockSpec((tm, d), lambda i: (i, 0)),
                  pl.BlockSpec((1, d), lambda i: (0, 0)),
                  mod.spec(3), mod.spec(4),
                  pl.BlockSpec((d, LANE), lambda i: (0, 0)),
                  pl.BlockSpec((d, LANE), lambda i: (0, 0)),
                  pl.BlockSpec((1, LANE), lambda i: (0, 0)),
                  pl.BlockSpec((tm, tm), lambda i: (0, 0))],
        out_specs=[pl.BlockSpec((tm, d), lambda i: (i, 0)),
                   pl.BlockSpec((tm, LANE), lambda i: (i, 0)),
                   pl.BlockSpec((8, LANE), lambda i: (0, 0))],
        out_shape=[jax.ShapeDtypeStruct((nt, d), F32),
                   jax.ShapeDtypeStruct((nt, LANE), F32),
                   jax.ShapeDtypeStruct((8, LANE), F32)],
        scratch_shapes=[pltpu.VMEM((8, LANE), F32)],
        compiler_params=_params(("arbitrary",), 4 * tm * d * 4 + 2 * tm * d * 2 + 2 * tm * tm * 2 + (8 << 20)),
        name="moe_route",
    )(x, gain.reshape(1, d), mod.table, mod.table, wr_hi, wr_lo, br, tri)


def _row_copy(src, dst, sem):
    return pltpu.make_async_copy(src, dst, sem)


def _dispatch_kernel(dest_hbm, y_ref, xs_in, xs_hbm, idx_smem, sem_idx, sem_rows):
    del xs_in
    i = pl.program_id(0)
    rows = y_ref.shape[0]
    idx_copy = pltpu.make_async_copy(dest_hbm.at[i], idx_smem, sem_idx)
    idx_copy.start()
    idx_copy.wait()

    def issue(t, carry):
        for s in range(TOP_K):
            d = idx_smem[TOP_K * t + s]
            _row_copy(y_ref.at[pl.ds(t, 1), :], xs_hbm.at[pl.ds(d, 1), :], sem_rows).start()
        return carry

    lax.fori_loop(0, rows, issue, 0)

    def drain(t, carry):
        for s in range(TOP_K):
            d = idx_smem[TOP_K * t + s]
            _row_copy(y_ref.at[pl.ds(t, 1), :], xs_hbm.at[pl.ds(d, 1), :], sem_rows).wait()
        return carry

    lax.fori_loop(0, rows, drain, 0)


def _dispatch(y, dest, n_rows, tmd):
    nt, d = y.shape
    xs0 = jnp.zeros((n_rows, d), F32)
    return pl.pallas_call(
        _dispatch_kernel,
        grid=(nt // tmd,),
        in_specs=[pl.BlockSpec(memory_space=pl.ANY),
                  pl.BlockSpec((tmd, d), lambda i: (i, 0)),
                  pl.BlockSpec(memory_space=pl.ANY)],
        out_specs=pl.BlockSpec(memory_space=pl.ANY),
        out_shape=jax.ShapeDtypeStruct((n_rows, d), F32),
        scratch_shapes=[pltpu.SMEM((TOP_K * tmd,), jnp.int32),
                        pltpu.SemaphoreType.DMA(()), pltpu.SemaphoreType.DMA(())],
        input_output_aliases={2: 0},
        compiler_params=_params(("arbitrary",), 2 * tmd * d * 4 + (8 << 20)),
        name="moe_dispatch",
    )(dest.reshape(nt // tmd, TOP_K * tmd), y, xs0)


def _expert_kernel(blk_ref, nused_ref, xs_ref, wg_ref, wu_ref, wd_ref, o_ref, y_scr):
    i, f = pl.program_id(0), pl.program_id(1)
    used = i < nused_ref[0]

    @pl.when(jnp.logical_and(used, f == 0))
    def _():
        _copy_rows(xs_ref, y_scr)

    @pl.when(used)
    def _():
        y = y_scr[...]
        hg = jnp.dot(y, wg_ref[0], preferred_element_type=F32)
        hu = jnp.dot(y, wu_ref[0], preferred_element_type=F32)
        h = (hg * _sigmoid(hg) * hu).astype(BF16)
        part = jnp.dot(h, wd_ref[0], preferred_element_type=F32)

        @pl.when(f == 0)
        def _():
            o_ref[...] = part

        @pl.when(f > 0)
        def _():
            o_ref[...] += part

    @pl.when(jnp.logical_and(jnp.logical_not(used), f == 0))
    def _():
        o_ref[...] = jnp.zeros_like(o_ref)


def _experts(xs, block_e, n_used, w_gate, w_up, w_down, tm):
    n_rows, d = xs.shape
    dff = w_gate.shape[2]
    tf = _pow2_tile(512, dff)
    vmem = 4 * tm * d * 4 + tm * d * 2 + 6 * d * tf * 2 + 5 * tm * tf * 4
    grid_spec = pltpu.PrefetchScalarGridSpec(
        num_scalar_prefetch=2,
        grid=(n_rows // tm, dff // tf),
        in_specs=[pl.BlockSpec((tm, d), lambda i, f, be, nu: (i, 0)),
                  pl.BlockSpec((1, d, tf), lambda i, f, be, nu: (be[i], 0, f)),
                  pl.BlockSpec((1, d, tf), lambda i, f, be, nu: (be[i], 0, f)),
                  pl.BlockSpec((1, tf, d), lambda i, f, be, nu: (be[i], f, 0))],
        out_specs=pl.BlockSpec((tm, d), lambda i, f, be, nu: (i, 0)),
        scratch_shapes=[pltpu.VMEM((tm, d), BF16)],
    )
    return pl.pallas_call(
        _expert_kernel,
        grid_spec=grid_spec,
        out_shape=jax.ShapeDtypeStruct((n_rows, d), F32),
        compiler_params=_params(("arbitrary", "arbitrary"), vmem + (4 << 20)),
        name="moe_experts",
    )(block_e, n_used, xs, w_gate, w_up, w_down)


def _combine_kernel(dest_hbm, ys_hbm, x_ref, gt_ref, info_ref, o_ref, idx_smem, buf, sem_idx, sem_rows):
    i = pl.program_id(0)
    rows = x_ref.shape[0]
    idx_copy = pltpu.make_async_copy(dest_hbm.at[i], idx_smem, sem_idx)
    idx_copy.start()
    idx_copy.wait()

    def issue(t, carry):
        for s in range(TOP_K):
            d = idx_smem[TOP_K * t + s]
            _row_copy(ys_hbm.at[pl.ds(d, 1), :], buf.at[s, pl.ds(t, 1), :], sem_rows).start()
        return carry

    lax.fori_loop(0, rows, issue, 0)

    def drain(t, carry):
        for s in range(TOP_K):
            d = idx_smem[TOP_K * t + s]
            _row_copy(ys_hbm.at[pl.ds(d, 1), :], buf.at[s, pl.ds(t, 1), :], sem_rows).wait()
        return carry

    lax.fori_loop(0, rows, drain, 0)
    info = info_ref[...]
    mix = info[:, 2:3] * buf[0] + info[:, 3:4] * buf[1]
    o_ref[...] = x_ref[...] + gt_ref[0] * mix


def _combine(ys, dest, x, mod, info, tmd):
    nt, d = x.shape
    return pl.pallas_call(
        _combine_kernel,
        grid=(nt // tmd,),
        in_specs=[pl.BlockSpec(memory_space=pl.ANY),
                  pl.BlockSpec(memory_space=pl.ANY),
                  pl.BlockSpec((tmd, d), lambda i: (i, 0)),
                  mod.spec(5),
                  pl.BlockSpec((tmd, LANE), lambda i: (i, 0))],
        out_specs=pl.BlockSpec((tmd, d), lambda i: (i, 0)),
        out_shape=jax.ShapeDtypeStruct((nt, d), F32),
        scratch_shapes=[pltpu.SMEM((TOP_K * tmd,), jnp.int32),
                        pltpu.VMEM((TOP_K, tmd, d), F32),
                        pltpu.SemaphoreType.DMA(()), pltpu.SemaphoreType.DMA(())],
        compiler_params=_params(("arbitrary",), 8 * tmd * d * 4 + (8 << 20)),
        name="moe_combine",
    )(dest.reshape(nt // tmd, TOP_K * tmd), ys, x, mod.table, info)


def _moe(x, gain, mod_route, mod_comb, w_router, b_router, w_gate, w_up, w_down, tm, tmd):
    nt, d = x.shape
    n_experts = w_router.shape[1]
    y, info, cnt = _route(x, gain, mod_route, w_router, b_router, tm)
    counts = cnt[0, :n_experts].astype(jnp.int32)
    padded = (counts + MOE_TILE - 1) // MOE_TILE * MOE_TILE
    pend = jnp.cumsum(padded)
    pstart = pend - padded
    n_rows = (-(-(nt * TOP_K) // MOE_TILE) + n_experts) * MOE_TILE
    nb = n_rows // MOE_TILE
    e_idx = info[:, 0:TOP_K].astype(jnp.int32)
    rank = info[:, 4:4 + TOP_K].astype(jnp.int32)
    dest = (pstart[e_idx] + rank).reshape(-1)
    block_e = jnp.minimum(jnp.searchsorted(pend, jnp.arange(nb, dtype=jnp.int32) * MOE_TILE, side='right'),
                          n_experts - 1).astype(jnp.int32)
    n_used = (pend[-1:] // MOE_TILE).astype(jnp.int32)
    xs = _dispatch(y, dest, n_rows, tmd)
    ys = _experts(xs, block_e, n_used, w_gate, w_up, w_down, MOE_TILE)
    return _combine(ys, dest, x, mod_comb, info, tmd)


def _attn_kernel(seq_len, sink_ref, q_ref, kp_ref, km_ref, kn_ref, kc_ref, vp_ref, vm_ref, vn_ref, vc_ref, o_ref):
    i = pl.program_id(1)
    tq = q_ref.shape[0]
    n_ctx = kc_ref.shape[0]
    span = tq + 2 * WINDOW
    q_pos = i * tq + lax.broadcasted_iota(jnp.int32, (tq, span + n_ctx), 0)
    col = lax.broadcasted_iota(jnp.int32, (tq, span + n_ctx), 1)
    k_pos = i * tq - WINDOW + col
    ok = (jnp.abs(q_pos - k_pos) <= WINDOW) & (k_pos >= 0) & (k_pos < seq_len)
    bias = jnp.where(ok | (col >= span), 0.0, -1e30).astype(F32)
    n_kv = km_ref.shape[1] // HEAD_DIM
    for kh in range(n_kv):
        hs = slice(kh * HEAD_DIM, (kh + 1) * HEAD_DIM)
        kcat = jnp.concatenate([kp_ref[:, hs], km_ref[:, hs], kn_ref[:, hs], kc_ref[:, hs]], axis=0)
        vcat = jnp.concatenate([vp_ref[:, hs], vm_ref[:, hs], vn_ref[:, hs], vc_ref[:, hs]], axis=0)
        for g in range(KV_GROUP):
            h = kh * KV_GROUP + g
            qs = slice(h * HEAD_DIM, (h + 1) * HEAD_DIM)
            s = lax.dot_general(q_ref[:, qs], kcat, (((1,), (1,)), ((), ())), preferred_element_type=F32) + bias
            sink = sink_ref[h]
            m = jnp.maximum(jnp.max(s, axis=-1, keepdims=True), sink)
            p = jnp.exp(s - m)
            denom = jnp.sum(p, axis=-1, keepdims=True) + jnp.exp(sink - m)
            o = jnp.dot(p.astype(BF16), vcat, preferred_element_type=F32) / denom
            o_ref[:, qs] = o.astype(o_ref.dtype)


def _attention(qkv, sink, n_batch, seq_len, ctx_len, d):
    tq = _pow2_tile(256, seq_len)
    kvw = d // KV_GROUP
    wb = WINDOW
    qb, sb = seq_len // tq, seq_len // wb
    kcol, vcol = d // kvw, d // kvw + 1
    n_lat = n_batch * seq_len
    ctx_block0 = n_lat // ctx_len

    def prev_map(col):
        return lambda b, i, s: (b * sb + jnp.maximum(i * (tq // wb) - 1, 0), col)

    def main_map(col):
        return lambda b, i, s: (b * qb + i, col)

    def next_map(col):
        return lambda b, i, s: (b * sb + jnp.minimum((i + 1) * (tq // wb), sb - 1), col)

    def ctx_map(col):
        return lambda b, i, s: (ctx_block0 + b, col)

    kv_specs = []
    for col in (kcol, vcol):
        kv_specs += [pl.BlockSpec((wb, kvw), prev_map(col)), pl.BlockSpec((tq, kvw), main_map(col)),
                     pl.BlockSpec((wb, kvw), next_map(col)), pl.BlockSpec((ctx_len, kvw), ctx_map(col))]
    grid_spec = pltpu.PrefetchScalarGridSpec(
        num_scalar_prefetch=1,
        grid=(n_batch, qb),
        in_specs=[pl.BlockSpec((tq, d), lambda b, i, s: (b * qb + i, 0))] + kv_specs,
        out_specs=pl.BlockSpec((tq, d), lambda b, i, s: (b * qb + i, 0)),
    )
    return pl.pallas_call(
        functools.partial(_attn_kernel, seq_len),
        grid_spec=grid_spec,
        out_shape=jax.ShapeDtypeStruct((n_lat, d), BF16),
        compiler_params=_params(("arbitrary", "arbitrary"), 32 << 20),
        name="attention",
    )(sink, *([qkv] * 9))


def _rope_tables(seq_len, tm):
    rows = seq_len // GRID_W
    row = np.repeat(np.arange(rows, dtype=np.float32), GRID_W)
    col = np.tile(np.arange(GRID_W, dtype=np.float32), rows)
    axis_dim = HEAD_DIM // 2
    inv_freq = (np.float32(ROPE_BASE) ** (-np.arange(0, axis_dim, 2, dtype=np.float32) / np.float32(axis_dim)))
    ang_r = (row[:, None] * inv_freq[None, :]).astype(np.float32)
    ang_c = (col[:, None] * inv_freq[None, :]).astype(np.float32)
    cos = np.concatenate([np.cos(ang_r), np.cos(ang_r), np.cos(ang_c), np.cos(ang_c)], axis=1)
    sin = np.concatenate([-np.sin(ang_r), np.sin(ang_r), -np.sin(ang_c), np.sin(ang_c)], axis=1)
    cos = np.concatenate([cos, np.ones((tm, HEAD_DIM), np.float32)], axis=0)
    sin = np.concatenate([sin, np.zeros((tm, HEAD_DIM), np.float32)], axis=0)
    return jnp.asarray(cos, F32), jnp.asarray(sin, F32)


def _chan_dft_kernel(x_ref, g_ref, sh_ref, sc_ref, cs_ref, a_ref, b_ref, y_scr):
    _norm_mod_rows(x_ref, g_ref, sh_ref, sc_ref, y_scr, BF16)
    gw = cs_ref.shape[0]
    cs = cs_ref[...]
    for g in range(x_ref.shape[1] // gw):
        cols = slice(g * gw, (g + 1) * gw)
        r = jnp.dot(y_scr[:, cols], cs, preferred_element_type=F32)
        a_ref[:, cols] = r[:, :gw].astype(a_ref.dtype)
        b_ref[:, cols] = r[:, gw:].astype(b_ref.dtype)


def _dft_mats(n):
    k = np.arange(n, dtype=np.int64)
    ang = 2.0 * np.pi * ((k[:, None] * k[None, :]) % n).astype(np.float64) / n
    scale = 1.0 / math.sqrt(n)
    return np.cos(ang) * scale, np.sin(ang) * scale


def _chan_dft(x, gain, mod, tm):
    nt, d = x.shape
    gw = d // FNET_GROUPS
    c, s = _dft_mats(gw)
    cs = jnp.asarray(np.concatenate([c, s], axis=1), BF16)
    return pl.pallas_call(
        _chan_dft_kernel,
        grid=(nt // tm,),
        in_specs=[pl.BlockSpec((tm, d), lambda i: (i, 0)),
                  pl.BlockSpec((1, d), lambda i: (0, 0)),
                  mod.spec(0), mod.spec(1),
                  pl.BlockSpec((gw, 2 * gw), lambda i: (0, 0))],
        out_specs=[pl.BlockSpec((tm, d), lambda i: (i, 0)), pl.BlockSpec((tm, d), lambda i: (i, 0))],
        out_shape=[jax.ShapeDtypeStruct((nt, d), BF16), jax.ShapeDtypeStruct((nt, d), BF16)],
        scratch_shapes=[pltpu.VMEM((tm, d), BF16)],
        compiler_params=_params(("arbitrary",), 2 * tm * d * 4 + 5 * tm * d * 2 + (8 << 20)),
        name="fnet_channel_dft",
    )(x, gain.reshape(1, d), mod.table, mod.table, cs)


def _pos_dft_kernel(cl_ref, sl_ref, a_ref, b_ref, o_ref):
    f = (jnp.dot(cl_ref[...], a_ref[...], preferred_element_type=F32)
         - jnp.dot(sl_ref[...], b_ref[...], preferred_element_type=F32))
    o_ref[...] = f.astype(o_ref.dtype)


def _pos_dft(a, b, n_batch, seq_len):
    nt, d = a.shape
    c, s = _dft_mats(seq_len)
    cl, sl = jnp.asarray(c, BF16), jnp.asarray(s, BF16)
    tm = _pow2_tile(512, seq_len)
    tn = _pow2_tile(512, d)
    mb = seq_len // tm
    return pl.pallas_call(
        _pos_dft_kernel,
        grid=(n_batch, d // tn, mb),
        in_specs=[pl.BlockSpec((tm, seq_len), lambda bb, n, m: (m, 0)),
                  pl.BlockSpec((tm, seq_len), lambda bb, n, m: (m, 0)),
                  pl.BlockSpec((seq_len, tn), lambda bb, n, m: (bb, n)),
                  pl.BlockSpec((seq_len, tn), lambda bb, n, m: (bb, n))],
        out_specs=pl.BlockSpec((tm, tn), lambda bb, n, m: (bb * mb + m, n)),
        out_shape=jax.ShapeDtypeStruct((nt, d), BF16),
        compiler_params=_params(("arbitrary", "arbitrary", "arbitrary"),
                                4 * tm * seq_len * 2 + 4 * seq_len * tn * 2 + 4 * tm * tn * 4 + (4 << 20)),
        name="fnet_position_dft",
    )(cl, sl, a, b)


def kernel(x, c, ctx, c_ctx, ada_w, ada_b, norm_mix_g, norm_ffn_g, gm_w_in, gm_v_g, gm_w_s, gm_b_s, gm_w_out, cv_w_pw1, cv_b_pw1, cv_w_dw, cv_b_dw, cv_norm_g, cv_w_pw2, at_w_qkv, at_q_g, at_k_g, at_sink, at_w_o, ft_w_out, f_w_gate, f_w_up, f_w_down, m_w_router, m_b_router, m_w_gate, m_w_up, m_w_down):
    n_batch, seq_len, d = x.shape
    ctx_len = ctx.shape[1]
    depth = ada_w.shape[0]
    assert depth == 4 and seq_len % GRID_W == 0 and ctx_len % CHUNK == 0 and seq_len % CHUNK == 0
    assert d % (HEAD_DIM * KV_GROUP) == 0 and n_batch < 16
    n_lat, n_ctx = n_batch * seq_len, n_batch * ctx_len
    nt = n_lat + n_ctx
    tm = _pow2_tile(ROW_TILE, seq_len, n_ctx)
    tn = _pow2_tile(COL_TILE, d // KV_GROUP)
    tmd = _pow2_tile(DMA_ROWS, seq_len, n_ctx)
    tms = _pow2_tile(512, seq_len, n_ctx)
    bf = lambda w: w.astype(BF16)

    cc = jnp.zeros((16, d), F32).at[:n_batch].set(c).at[n_batch].set(c_ctx)
    table = _ada_table(cc, ada_w, ada_b).reshape(depth, 16 * 6, 1, d)
    mods = [_Mod(table[l], tm, seq_len, n_batch) for l in range(depth)]
    mods_s = [_Mod(table[l], tms, seq_len, n_batch) for l in range(depth)]
    mods_d = [_Mod(table[l], tmd, seq_len, n_batch) for l in range(depth)]

    xs = jnp.concatenate([x.reshape(n_lat, d), ctx.reshape(n_ctx, d)], axis=0)

    h = _proj_call(_proj_gelu_kernel, xs, norm_mix_g[0], mods[0], 0, [(bf(gm_w_in[0]), 0)], [], [],
                   2 * d, tm, tn, "gmlp_in")
    xs = _gmlp_tail(h, gm_v_g[0], gm_w_s[0], gm_b_s[0], bf(gm_w_out[0]), xs, mods_s[0], tms)
    xs = _ffn(xs, norm_ffn_g[0], mods[0], bf(f_w_gate[0]), bf(f_w_up[0]), bf(f_w_down[0]), tm)

    w_pw1 = bf(cv_w_pw1[0])
    b_pw1 = cv_b_pw1[0].reshape(1, 2 * d)
    bias_specs = [pl.BlockSpec((1, tn), lambda i, j: (0, j)),
                  pl.BlockSpec((1, tn), lambda i, j: (0, j + d // tn))]
    z = _proj_call(_proj_glu_kernel, xs, norm_mix_g[1], mods[1], 0, [(w_pw1, 0), (w_pw1, d // tn)],
                   [b_pw1, b_pw1], bias_specs, d, tm, tn, "conv_in")
    tmc = _pow2_tile(256, seq_len, ctx_len)
    xs = _conv_tail(z, cv_w_dw[0], cv_b_dw[0], cv_norm_g[0], bf(cv_w_pw2[0]), xs,
                    _Mod(table[1], tmc, seq_len, n_batch), tmc, seq_len, n_lat, ctx_len)
    xs = _moe(xs, norm_ffn_g[1], mods[1], mods_d[1], m_w_router[0], m_b_router[0],
              bf(m_w_gate[0]), bf(m_w_up[0]), bf(m_w_down[0]), tm, tmd)

    qkv_dim = at_w_qkv.shape[2]
    n_qk_tiles = (d + d // KV_GROUP) // tn
    cos, sin = _rope_tables(seq_len, tm)
    rope_block = lambda i, j: (jnp.where(i < n_lat // tm, i % (seq_len // tm), seq_len // tm), 0)
    q_gain = at_q_g[0] * (HEAD_DIM ** -0.5)
    gains = jnp.stack([q_gain] * (d // tn) + [at_k_g[0]] * (qkv_dim // tn - d // tn)).reshape(-1, 1, HEAD_DIM)
    qkv_specs = [pl.BlockSpec((1, 1, HEAD_DIM), lambda i, j: (j, 0, 0)),
                 pl.BlockSpec((tm, HEAD_DIM), rope_block), pl.BlockSpec((tm, HEAD_DIM), rope_block)]
    qkv = _proj_call(functools.partial(_proj_qkv_kernel, n_qk_tiles), xs, norm_mix_g[2], mods[2], 0,
                     [(bf(at_w_qkv[0]), 0)], [gains, cos, sin], qkv_specs, qkv_dim, tm, tn, "qkv")
    o = _attention(qkv, at_sink[0], n_batch, seq_len, ctx_len, d)
    xl = _resid_mm(o, bf(at_w_o[0]), xs, mods_s[2], 2, n_lat, tms, "attn_out")
    xl = _ffn(xl, norm_ffn_g[2], mods[2], bf(f_w_gate[1]), bf(f_w_up[1]), bf(f_w_down[1]), tm)

    a, b = _chan_dft(xl, norm_mix_g[3], mods[3], tm)
    f = _pos_dft(a, b, n_batch, seq_len)
    xl = _resid_mm(f, bf(ft_w_out[0]), xl, mods_s[3], 2, n_lat, tms, "fnet_out")
    xl = _moe(xl, norm_ffn_g[3], mods[3], mods_d[3], m_w_router[1], m_b_router[1],
              bf(m_w_gate[1]), bf(m_w_up[1]), bf(m_w_down[1]), tm, tmd)
    return xl.reshape(n_batch, seq_len, d)
```

```python
import functools
import math

import numpy as np
import jax
import jax.numpy as jnp
from jax import lax
from jax.experimental import pallas as pl
from jax.experimental.pallas import tpu as pltpu

F32 = jnp.float32
BF16 = jnp.bfloat16

EPS = 1e-6
GRID_W = 64
CHUNK = 128
GMLP_GROUPS = 8
CONV_WIDTH = 31
CONV_PAD = CONV_WIDTH // 2
HEAD_DIM = 128
KV_GROUP = 4
WINDOW = 128
ROPE_BASE = 10000.0
FNET_GROUPS = 8
TOP_K = 2
LOG2E = math.log2(math.e)

LANE = 128
V7X_VMEM_LIMIT = 56 << 20

ROW_TILE = 1024
COL_TILE = 512
CONV_HALO = 16
MOE_TILE = 1024
DMA_ROWS = 512
ZERO_ROWS = 128


def _params(sem, vmem_bytes):
    return pltpu.CompilerParams(dimension_semantics=sem,
                                vmem_limit_bytes=int(min(max(vmem_bytes, 16 << 20), V7X_VMEM_LIMIT)))


def _pow2_tile(target, *sizes):
    t = target
    while any(s % t for s in sizes):
        t //= 2
    return t


def _sigmoid(x):
    return 1.0 / (1.0 + jnp.exp(-x))


def _norm_mod(x, g, shift, scale):
    ms = jnp.mean(x * x, axis=-1, keepdims=True)
    return (x * lax.rsqrt(ms + EPS) * g) * (1.0 + scale) + shift


def _norm_mod_rows(x_ref, g_ref, sh_ref, sc_ref, dst_ref, dtype, chunk=256):
    rows, d = x_ref.shape
    chunk = min(chunk, rows)
    schunk = min(4 * chunk, rows)
    gs = g_ref[...] * (1.0 + sc_ref[0])
    sh = sh_ref[0]

    def run(rs_scr):
        def stats(r, carry):
            sl = pl.ds(pl.multiple_of(r * schunk, schunk), schunk)
            xv = x_ref[sl, :]
            rs_scr[sl, :] = lax.rsqrt(jnp.sum(xv * xv, axis=-1, keepdims=True) * (1.0 / d) + EPS)
            return carry

        lax.fori_loop(0, rows // schunk, stats, 0)

        def apply(r, carry):
            sl = pl.ds(pl.multiple_of(r * chunk, chunk), chunk)
            dst_ref[sl, :] = (x_ref[sl, :] * rs_scr[sl, :] * gs + sh).astype(dtype)
            return carry

        lax.fori_loop(0, rows // chunk, apply, 0)

    pl.run_scoped(run, pltpu.VMEM((rows, 1), F32))


def _ada_kernel(c_ref, w_ref, b_ref, o_ref):
    c = c_ref[...]
    s = (c * _sigmoid(c)).astype(BF16)
    o_ref[0] = jnp.dot(s, w_ref[0].astype(BF16), preferred_element_type=F32) + b_ref[0]


def _ada_table(cc, ada_w, ada_b):
    depth, d, n6 = ada_w.shape
    tn = _pow2_tile(1024, n6)
    rows = cc.shape[0]
    return pl.pallas_call(
        _ada_kernel,
        grid=(depth, n6 // tn),
        in_specs=[pl.BlockSpec((rows, d), lambda l, j: (0, 0)),
                  pl.BlockSpec((1, d, tn), lambda l, j: (l, 0, j)),
                  pl.BlockSpec((1, 1, tn), lambda l, j: (l, 0, j))],
        out_specs=pl.BlockSpec((1, rows, tn), lambda l, j: (l, 0, j)),
        out_shape=jax.ShapeDtypeStruct((depth, rows, n6), F32),
        compiler_params=_params(("arbitrary", "arbitrary"), 2 * d * tn * 4 + d * tn * 2 + (8 << 20)),
        name="ada_table",
    )(cc, ada_w, ada_b.reshape(depth, 1, n6))


class _Mod:
    def __init__(self, table, tm, seq_len, n_batch):
        self.table, self.tm, self.seq_len, self.n_batch = table, tm, seq_len, n_batch
        self.d = table.shape[-1]

    def spec(self, chunk):
        tm, seq_len, n_batch = self.tm, self.seq_len, self.n_batch

        def index(i, *_):
            return (jnp.minimum(i * tm // seq_len, n_batch) * 6 + chunk, 0, 0)

        return pl.BlockSpec((1, 1, self.d), index)


def _gelu_tanh(x):
    return 0.5 * x * (1.0 + jnp.tanh(math.sqrt(2.0 / math.pi) * (x + 0.044715 * (x * x * x))))


def _proj_gelu_kernel(n_lat_blocks, xl_ref, xc_ref, g_ref, sh_ref, sc_ref, w_ref, o_ref, y_scr):
    i, j = pl.program_id(0), pl.program_id(1)

    @pl.when(jnp.logical_and(j == 0, i < n_lat_blocks))
    def _():
        _norm_mod_rows(xl_ref, g_ref, sh_ref, sc_ref, y_scr, BF16)

    @pl.when(jnp.logical_and(j == 0, i >= n_lat_blocks))
    def _():
        _norm_mod_rows(xc_ref, g_ref, sh_ref, sc_ref, y_scr, BF16)

    acc = jnp.dot(y_scr[...], w_ref[...], preferred_element_type=F32)
    o_ref[...] = _gelu_tanh(acc).astype(o_ref.dtype)


def _proj_glu_kernel(x_ref, g_ref, sh_ref, sc_ref, wa_ref, wg_ref, ba_ref, bg_ref, o_ref, y_scr):
    @pl.when(pl.program_id(1) == 0)
    def _():
        _norm_mod_rows(x_ref, g_ref, sh_ref, sc_ref, y_scr, BF16)

    y = y_scr[...]
    a = jnp.dot(y, wa_ref[...], preferred_element_type=F32) + ba_ref[...]
    gate = jnp.dot(y, wg_ref[...], preferred_element_type=F32) + bg_ref[...]
    o_ref[...] = (a * _sigmoid(gate)).astype(o_ref.dtype)


def _rope_layout(t, n_heads):
    lead = t.shape[:-1]
    q = HEAD_DIM // 4
    t = t.reshape(*lead, n_heads, 2, 2, q)
    return jnp.swapaxes(t, -2, -3).reshape(*lead, n_heads * HEAD_DIM)


def _rope_partner(x):
    return pltpu.roll(x, HEAD_DIM // 2, 1)


def _proj_qkv_kernel(n_qk_tiles, x_ref, g_ref, sh_ref, sc_ref, w_ref, hg_ref, cos_ref, sin_ref, o_ref, y_scr):
    j = pl.program_id(1)

    @pl.when(j == 0)
    def _():
        _norm_mod_rows(x_ref, g_ref, sh_ref, sc_ref, y_scr, BF16)

    acc = jnp.dot(y_scr[...], w_ref[...], preferred_element_type=F32)

    @pl.when(j < n_qk_tiles)
    def _():
        hg, cos, sin = hg_ref[0], cos_ref[...], sin_ref[...]
        for h in range(acc.shape[1] // HEAD_DIM):
            t = acc[:, h * HEAD_DIM:(h + 1) * HEAD_DIM]
            t = t * lax.rsqrt(jnp.mean(t * t, axis=-1, keepdims=True) + EPS) * hg
            t = t * cos + _rope_partner(t) * sin
            o_ref[:, h * HEAD_DIM:(h + 1) * HEAD_DIM] = t.astype(o_ref.dtype)

    @pl.when(j >= n_qk_tiles)
    def _():
        o_ref[...] = acc.astype(o_ref.dtype)


def _proj_call(kernel, x, gain, mod, chunk0, weights, extra_in, extra_specs, n_out, tm, tn, name):
    if isinstance(x, tuple):
        x_lat, x_ctx = x
        nlb = x_lat.shape[0] // tm
        nt, d = x_lat.shape[0] + x_ctx.shape[0], x_lat.shape[1]
        in_specs = [pl.BlockSpec((tm, d), lambda i, j: (jnp.minimum(i, nlb - 1), 0)),
                    pl.BlockSpec((tm, d), lambda i, j: (jnp.maximum(i - nlb, 0), 0))]
        args = [x_lat, x_ctx]
    else:
        nt, d = x.shape
        in_specs = [pl.BlockSpec((tm, d), lambda i, j: (i, 0))]
        args = [x]
    in_specs += [pl.BlockSpec((1, d), lambda i, j: (0, 0)), mod.spec(chunk0), mod.spec(chunk0 + 1)]
    args += [gain.reshape(1, d), mod.table, mod.table]
    for w, off in weights:
        in_specs.append(pl.BlockSpec((d, tn), functools.partial(lambda i, j, off: (0, j + off), off=off)))
        args.append(w)
    in_specs += extra_specs
    args += extra_in
    vmem = (2 * len(args[:2 if isinstance(x, tuple) else 1]) * tm * d * 4 + tm * d * 2
            + len(weights) * 2 * d * tn * 2 + 2 * tm * tn * 2 + 6 * tm * tn * 4)
    return pl.pallas_call(
        kernel,
        grid=(nt // tm, n_out // tn),
        in_specs=in_specs,
        out_specs=pl.BlockSpec((tm, tn), lambda i, j: (i, j)),
        out_shape=jax.ShapeDtypeStruct((nt, n_out), BF16),
        scratch_shapes=[pltpu.VMEM((tm, d), BF16)],
        compiler_params=_params(("arbitrary", "arbitrary"), vmem + (4 << 20)),
        name=name,
    )(*args)


def _resid_mm_kernel(a_ref, w_ref, x_ref, gt_ref, o_ref):
    o_ref[...] = x_ref[...] + gt_ref[0] * jnp.dot(a_ref[...], w_ref[...], preferred_element_type=F32)


def _resid_mm(a, w, x, mod, gate_chunk, n_rows, tm, name):
    k, d = w.shape
    return pl.pallas_call(
        _resid_mm_kernel,
        grid=(n_rows // tm,),
        in_specs=[pl.BlockSpec((tm, k), lambda i: (i, 0)),
                  pl.BlockSpec((k, d), lambda i: (0, 0)),
                  pl.BlockSpec((tm, d), lambda i: (i, 0)),
                  mod.spec(gate_chunk)],
        out_specs=pl.BlockSpec((tm, d), lambda i: (i, 0)),
        out_shape=jax.ShapeDtypeStruct((n_rows, d), F32),
        compiler_params=_params(("arbitrary",), 2 * tm * k * 2 + 2 * k * d * 2 + 5 * tm * d * 4 + (4 << 20)),
        name=name,
    )(a, w, x, mod.table)


def _gmlp_tail_kernel(n_lat_blocks, h_ref, vg_ref, ws_ref, bs_ref, w_ref, xl_ref, xc_ref, gt_ref, o_ref, z_scr):
    tm, d = xl_ref.shape
    gw = d // GMLP_GROUPS
    v = h_ref[:, d:].astype(F32)
    vn = (v * lax.rsqrt(jnp.mean(v * v, axis=-1, keepdims=True) + EPS) * vg_ref[...]).astype(BF16)
    for c in range(tm // CHUNK):
        rows = slice(c * CHUNK, (c + 1) * CHUNK)
        for g in range(GMLP_GROUPS):
            cols = slice(g * gw, (g + 1) * gw)
            sv = jnp.dot(ws_ref[g], vn[rows, cols], preferred_element_type=F32) + bs_ref[g]
            z_scr[rows, cols] = (h_ref[rows, cols].astype(F32) * sv).astype(BF16)
    x = jnp.where(pl.program_id(0) < n_lat_blocks, xl_ref[...], xc_ref[...])
    o_ref[...] = x + gt_ref[0] * jnp.dot(z_scr[...], w_ref[...], preferred_element_type=F32)


def _gmlp_tail(h, v_g, w_s, b_s, w_out, x_lat, x_ctx, mod, tm):
    d = x_lat.shape[1]
    nt = x_lat.shape[0] + x_ctx.shape[0]
    nlb = x_lat.shape[0] // tm
    bs_tile = jnp.broadcast_to(b_s[:, :, None], (GMLP_GROUPS, CHUNK, d // GMLP_GROUPS)).astype(F32)
    return pl.pallas_call(
        functools.partial(_gmlp_tail_kernel, nlb),
        grid=(nt // tm,),
        in_specs=[pl.BlockSpec((tm, 2 * d), lambda i: (i, 0)),
                  pl.BlockSpec((1, d), lambda i: (0, 0)),
                  pl.BlockSpec((GMLP_GROUPS, CHUNK, CHUNK), lambda i: (0, 0, 0)),
                  pl.BlockSpec((GMLP_GROUPS, CHUNK, d // GMLP_GROUPS), lambda i: (0, 0, 0)),
                  pl.BlockSpec((d, d), lambda i: (0, 0)),
                  pl.BlockSpec((tm, d), lambda i: (jnp.minimum(i, nlb - 1), 0)),
                  pl.BlockSpec((tm, d), lambda i: (jnp.maximum(i - nlb, 0), 0)),
                  mod.spec(2)],
        out_specs=pl.BlockSpec((tm, d), lambda i: (i, 0)),
        out_shape=jax.ShapeDtypeStruct((nt, d), F32),
        scratch_shapes=[pltpu.VMEM((tm, d), BF16)],
        compiler_params=_params(("arbitrary",), 2 * tm * 2 * d * 2 + 2 * d * d * 2 + 8 * tm * d * 4 + (6 << 20)),
        name="gmlp_tail",
    )(h, v_g.reshape(1, d), w_s.astype(BF16), bs_tile, w_out, x_lat, x_ctx, mod.table)


def _conv_tail_kernel(edges, z_ref, zp_ref, zn_ref, wdw_ref, bdw_ref, ng_ref, w_ref, x_ref, gt_ref,
                      o_ref, ext_scr, cv_scr, sh_scr):
    i = pl.program_id(0)
    tm, d = x_ref.shape
    first, last = edges(i)
    ext_scr[0:CONV_HALO, :] = jnp.where(first, 0.0, zp_ref[...].astype(F32))
    ext_scr[CONV_HALO:CONV_HALO + tm, :] = z_ref[...].astype(F32)
    ext_scr[CONV_HALO + tm:, :] = jnp.where(last, 0.0, zn_ref[...].astype(F32))

    rc = min(64, tm)
    cc = sh_scr.shape[2]
    sh_rows = sh_scr.shape[1]

    def col_body(c, carry):
        cols = pl.ds(pl.multiple_of(c * cc, cc), cc)
        wts = wdw_ref[:, cols]
        bias = bdw_ref[:, cols]
        for b in range(1, 8):
            sh_scr[b] = ext_scr[b:b + sh_rows, cols]
        for r in range(tm // rc):
            acc = jnp.zeros((rc, cc), F32) + bias
            for k in range(CONV_WIDTH):
                off = r * rc + CONV_HALO - CONV_PAD + k
                b, a = off % 8, off - off % 8
                src = ext_scr[a:a + rc, cols] if b == 0 else sh_scr[b, a:a + rc, :]
                acc = acc + wts[k:k + 1, :] * src
            cv_scr[r * rc:(r + 1) * rc, cols] = acc
        return carry

    lax.fori_loop(0, d // cc, col_body, 0)
    cv = cv_scr[...]
    t = cv * lax.rsqrt(jnp.mean(cv * cv, axis=-1, keepdims=True) + EPS) * ng_ref[...]
    t = (t * _sigmoid(t)).astype(BF16)
    o_ref[...] = x_ref[...] + gt_ref[0] * jnp.dot(t, w_ref[...], preferred_element_type=F32)


def _conv_tail(z, w_dw, b_dw, n_g, w_pw2, x, mod, tm, seq_len, n_lat, ctx_len):
    nt, d = x.shape
    hb = tm // CONV_HALO
    n_halo_blocks = nt // CONV_HALO
    lat_blocks, seq_blocks, ctx_blocks = n_lat // tm, seq_len // tm, ctx_len // tm

    def edges(i):
        in_lat = i < lat_blocks
        pos = jnp.where(in_lat, i % seq_blocks, (i - lat_blocks) % ctx_blocks)
        per = jnp.where(in_lat, seq_blocks, ctx_blocks)
        return pos == 0, pos == per - 1

    kdw = w_dw.shape[0]
    kpad = -(-kdw // 8) * 8
    w_dw_p = jnp.zeros((kpad, d), F32).at[:kdw].set(w_dw)
    return pl.pallas_call(
        functools.partial(_conv_tail_kernel, edges),
        grid=(nt // tm,),
        in_specs=[pl.BlockSpec((tm, d), lambda i: (i, 0)),
                  pl.BlockSpec((CONV_HALO, d), lambda i: (jnp.maximum(i * hb - 1, 0), 0)),
                  pl.BlockSpec((CONV_HALO, d), lambda i: (jnp.minimum((i + 1) * hb, n_halo_blocks - 1), 0)),
                  pl.BlockSpec((kpad, d), lambda i: (0, 0)),
                  pl.BlockSpec((1, d), lambda i: (0, 0)),
                  pl.BlockSpec((1, d), lambda i: (0, 0)),
                  pl.BlockSpec((d, d), lambda i: (0, 0)),
                  pl.BlockSpec((tm, d), lambda i: (i, 0)),
                  mod.spec(2)],
        out_specs=pl.BlockSpec((tm, d), lambda i: (i, 0)),
        out_shape=jax.ShapeDtypeStruct((nt, d), F32),
        scratch_shapes=[pltpu.VMEM((tm + 2 * CONV_HALO, d), F32), pltpu.VMEM((tm, d), F32),
                        pltpu.VMEM((8, tm + 2 * CONV_HALO - 8, min(256, d)), F32)],
        compiler_params=_params(("arbitrary",), 2 * d * d * 2 + 12 * tm * d * 4 + (6 << 20)),
        name="conv_tail",
    )(z, z, z, w_dw_p, b_dw.reshape(1, d), n_g.reshape(1, d), w_pw2, x, mod.table)


def _copy_rows(src_ref, dst_ref, chunk=256):
    rows = src_ref.shape[0]
    chunk = min(chunk, rows)

    def body(r, carry):
        sl = pl.ds(pl.multiple_of(r * chunk, chunk), chunk)
        dst_ref[sl, :] = src_ref[sl, :].astype(dst_ref.dtype)
        return carry

    lax.fori_loop(0, rows // chunk, body, 0)


def _swiglu_partial(y, wg, wu, wd):
    hg = jnp.dot(y, wg, preferred_element_type=F32)
    hu = jnp.dot(y, wu, preferred_element_type=F32)
    h = (hg * _sigmoid(hg) * hu).astype(BF16)
    return jnp.dot(h, wd, preferred_element_type=F32)


def _ffn_kernel(x_ref, g_ref, sh_ref, sc_ref, gt_ref, wg_ref, wu_ref, wd_ref, o_ref, y_scr):
    f = pl.program_id(1)

    @pl.when(f == 0)
    def _():
        _norm_mod_rows(x_ref, g_ref, sh_ref, sc_ref, y_scr, BF16)
        o_ref[...] = x_ref[...] + gt_ref[0] * _swiglu_partial(y_scr[...], wg_ref[...], wu_ref[...], wd_ref[...])

    @pl.when(f > 0)
    def _():
        o_ref[...] += gt_ref[0] * _swiglu_partial(y_scr[...], wg_ref[...], wu_ref[...], wd_ref[...])


def _ffn(x, gain, mod, w_gate, w_up, w_down, tm):
    nt, d = x.shape
    dff = w_gate.shape[1]
    tf = 512 if dff % 512 == 0 else 256
    while dff % tf:
        tf //= 2
    vmem = 4 * tm * d * 4 + tm * d * 2 + 6 * d * tf * 2 + 5 * tm * tf * 4
    return pl.pallas_call(
        _ffn_kernel,
        grid=(nt // tm, dff // tf),
        in_specs=[pl.BlockSpec((tm, d), lambda i, f: (i, 0)),
                  pl.BlockSpec((1, d), lambda i, f: (0, 0)),
                  mod.spec(3), mod.spec(4), mod.spec(5),
                  pl.BlockSpec((d, tf), lambda i, f: (0, f)),
                  pl.BlockSpec((d, tf), lambda i, f: (0, f)),
                  pl.BlockSpec((tf, d), lambda i, f: (f, 0))],
        out_specs=pl.BlockSpec((tm, d), lambda i, f: (i, 0)),
        out_shape=jax.ShapeDtypeStruct((nt, d), F32),
        scratch_shapes=[pltpu.VMEM((tm, d), BF16)],
        compiler_params=_params(("arbitrary", "arbitrary"), vmem + (4 << 20)),
        name="ffn_dense",
    )(x, gain.reshape(1, d), mod.table, mod.table, mod.table, w_gate, w_up, w_down)


def _route_kernel(n_experts, x_ref, g_ref, sh_ref, sc_ref, wh_ref, wl_ref, br_ref, tri_ref,
                  y_ref, info_ref, cnt_ref, carry):
    i = pl.program_id(0)

    @pl.when(i == 0)
    def _():
        carry[...] = jnp.zeros_like(carry)

    _norm_mod_rows(x_ref, g_ref, sh_ref, sc_ref, y_ref, F32)
    y = y_ref[...]
    yh = y.astype(BF16)
    yl = (y - yh.astype(F32)).astype(BF16)
    wh, wl = wh_ref[...], wl_ref[...]
    lg = (jnp.dot(yh, wh, preferred_element_type=F32) + jnp.dot(yh, wl, preferred_element_type=F32)
          + jnp.dot(yl, wh, preferred_element_type=F32) + br_ref[...])
    lane = lax.broadcasted_iota(jnp.int32, lg.shape, 1)
    neg = jnp.float32(-jnp.inf)
    lg = jnp.where(lane < n_experts, lg, neg)
    m1 = jnp.max(lg, axis=-1, keepdims=True)
    i1 = jnp.min(jnp.where(lg == m1, lane, LANE), axis=-1, keepdims=True)
    lg2 = jnp.where(lane == i1, neg, lg)
    m2 = jnp.max(lg2, axis=-1, keepdims=True)
    i2 = jnp.min(jnp.where(lg2 == m2, lane, LANE), axis=-1, keepdims=True)
    e = jnp.exp(m2 - m1)
    w1 = 1.0 / (1.0 + e)
    w2 = e * w1
    hit1, hit2 = lane == i1, lane == i2
    onehot = (hit1 | hit2).astype(F32)
    pre = jnp.dot(tri_ref[...], onehot.astype(BF16), preferred_element_type=F32) + carry[0:1, :]
    r1 = jnp.sum(jnp.where(hit1, pre, 0.0), axis=-1, keepdims=True)
    r2 = jnp.sum(jnp.where(hit2, pre, 0.0), axis=-1, keepdims=True)
    total = carry[0:1, :] + jnp.sum(onehot, axis=0, keepdims=True)
    carry[...] = jnp.broadcast_to(total, carry.shape)
    cnt_ref[...] = jnp.broadcast_to(total, cnt_ref.shape)
    info = jnp.where(lane == 0, i1.astype(F32), 0.0)
    info = jnp.where(lane == 1, i2.astype(F32), info)
    info = jnp.where(lane == 2, w1, info)
    info = jnp.where(lane == 3, w2, info)
    info = jnp.where(lane == 4, r1, info)
    info = jnp.where(lane == 5, r2, info)
    info_ref[...] = info


def _route(x, gain, mod, w_router, b_router, tm):
    nt, d = x.shape
    n_experts = w_router.shape[1]
    wr = jnp.zeros((d, LANE), F32).at[:, :n_experts].set(w_router)
    wr_hi = wr.astype(BF16)
    wr_lo = (wr - wr_hi.astype(F32)).astype(BF16)
    br = jnp.zeros((1, LANE), F32).at[0, :n_experts].set(b_router)
    tri = jnp.tril(jnp.ones((tm, tm), BF16), -1)
    return pl.pallas_call(
        functools.partial(_route_kernel, n_experts),
        grid=(nt // tm,),
        in_specs=[pl.BlockSpec((tm, d), lambda i: (i, 0)),
                  pl.BlockSpec((1, d), lambda i: (0, 0)),
                  mod.spec(3), mod.spec(4),
                  pl.BlockSpec((d, LANE), lambda i: (0, 0)),
                  pl.BlockSpec((d, LANE), lambda i: (0, 0)),
                  pl.BlockSpec((1, LANE), lambda i: (0, 0)),
                  pl.BlockSpec((tm, tm), lambda i: (0, 0))],
        out_specs=[pl.BlockSpec((tm, d), lambda i: (i, 0)),
                   pl.BlockSpec((tm, LANE), lambda i: (i, 0)),
                   pl.BlockSpec((8, LANE), lambda i: (0, 0))],
        out_shape=[jax.ShapeDtypeStruct((nt, d), F32),
                   jax.ShapeDtypeStruct((nt, LANE), F32),
                   jax.ShapeDtypeStruct((8, LANE), F32)],
        scratch_shapes=[pltpu.VMEM((8, LANE), F32)],
        compiler_params=_params(("arbitrary",), 4 * tm * d * 4 + 2 * tm * d * 2 + 2 * tm * tm * 2 + (8 << 20)),
        name="moe_route",
    )(x, gain.reshape(1, d), mod.table, mod.table, wr_hi, wr_lo, br, tri)


def _row_copy(src, dst, sem):
    return pltpu.make_async_copy(src, dst, sem)


def _dispatch_kernel(pstart_ref, pend_ref, nused_ref, dest_hbm, y_ref, xs_hbm, idx_smem, zbuf, sem_idx, sem_rows,
                     sem_zero):
    i = pl.program_id(0)
    rows = y_ref.shape[0]
    zr = zbuf.shape[0]
    n_blocks = xs_hbm.shape[0] // MOE_TILE

    @pl.when(i == 0)
    def _():
        zbuf[...] = jnp.zeros_like(zbuf)

        def zero_copy(block_start, c):
            start = pl.multiple_of(block_start + c * zr, zr)
            return pltpu.make_async_copy(zbuf, xs_hbm.at[pl.ds(start, zr), :], sem_zero)

        def zero_block(block_start):
            for c in range(MOE_TILE // zr):
                zero_copy(block_start, c).start()
            for c in range(MOE_TILE // zr):
                zero_copy(block_start, c).wait()

        for e in range(pstart_ref.shape[0]):
            @pl.when(pend_ref[e] > pstart_ref[e])
            def _():
                zero_block(pend_ref[e] - MOE_TILE)

        def tail(blk, carry):
            zero_block(blk * MOE_TILE)
            return carry

        lax.fori_loop(nused_ref[0], n_blocks, tail, 0)

    idx_copy = pltpu.make_async_copy(dest_hbm.at[i], idx_smem, sem_idx)
    idx_copy.start()
    idx_copy.wait()

    def issue(t, carry):
        for s in range(TOP_K):
            d = idx_smem[TOP_K * t + s]
            _row_copy(y_ref.at[pl.ds(t, 1), :], xs_hbm.at[pl.ds(d, 1), :], sem_rows).start()
        return carry

    lax.fori_loop(0, rows, issue, 0)

    def drain(t, carry):
        for s in range(TOP_K):
            d = idx_smem[TOP_K * t + s]
            _row_copy(y_ref.at[pl.ds(t, 1), :], xs_hbm.at[pl.ds(d, 1), :], sem_rows).wait()
        return carry

    lax.fori_loop(0, rows, drain, 0)


def _dispatch(y, dest, pstart, pend, n_used, n_rows, tmd):
    nt, d = y.shape
    grid_spec = pltpu.PrefetchScalarGridSpec(
        num_scalar_prefetch=3,
        grid=(nt // tmd,),
        in_specs=[pl.BlockSpec(memory_space=pl.ANY),
                  pl.BlockSpec((tmd, d), lambda i, ps, pe, nu: (i, 0))],
        out_specs=pl.BlockSpec(memory_space=pl.ANY),
        scratch_shapes=[pltpu.SMEM((TOP_K * tmd,), jnp.int32),
                        pltpu.VMEM((ZERO_ROWS, d), F32),
                        pltpu.SemaphoreType.DMA(()), pltpu.SemaphoreType.DMA(()), pltpu.SemaphoreType.DMA(())],
    )
    return pl.pallas_call(
        _dispatch_kernel,
        grid_spec=grid_spec,
        out_shape=jax.ShapeDtypeStruct((n_rows, d), F32),
        compiler_params=_params(("arbitrary",), 2 * tmd * d * 4 + (8 << 20)),
        name="moe_dispatch",
    )(pstart, pend, n_used, dest.reshape(nt // tmd, TOP_K * tmd), y)


def _expert_kernel(blk_ref, nused_ref, xs_ref, wg_ref, wu_ref, wd_ref, o_ref, y_scr):
    i, f = pl.program_id(0), pl.program_id(1)
    used = i < nused_ref[0]

    @pl.when(jnp.logical_and(used, f == 0))
    def _():
        _copy_rows(xs_ref, y_scr)
        o_ref[...] = _swiglu_partial(y_scr[...], wg_ref[0], wu_ref[0], wd_ref[0])

    @pl.when(jnp.logical_and(used, f > 0))
    def _():
        o_ref[...] += _swiglu_partial(y_scr[...], wg_ref[0], wu_ref[0], wd_ref[0])

    @pl.when(jnp.logical_and(jnp.logical_not(used), f == 0))
    def _():
        o_ref[...] = jnp.zeros_like(o_ref)


def _experts(xs, block_e, n_used, w_gate, w_up, w_down, tm):
    n_rows, d = xs.shape
    dff = w_gate.shape[2]
    tf = _pow2_tile(512, dff)
    vmem = 4 * tm * d * 4 + tm * d * 2 + 6 * d * tf * 2 + 5 * tm * tf * 4
    grid_spec = pltpu.PrefetchScalarGridSpec(
        num_scalar_prefetch=2,
        grid=(n_rows // tm, dff // tf),
        in_specs=[pl.BlockSpec((tm, d), lambda i, f, be, nu: (jnp.minimum(i, nu[0] - 1), 0)),
                  pl.BlockSpec((1, d, tf), lambda i, f, be, nu: (be[i], 0, f)),
                  pl.BlockSpec((1, d, tf), lambda i, f, be, nu: (be[i], 0, f)),
                  pl.BlockSpec((1, tf, d), lambda i, f, be, nu: (be[i], f, 0))],
        out_specs=pl.BlockSpec((tm, d), lambda i, f, be, nu: (i, 0)),
        scratch_shapes=[pltpu.VMEM((tm, d), BF16)],
    )
    return pl.pallas_call(
        _expert_kernel,
        grid_spec=grid_spec,
        out_shape=jax.ShapeDtypeStruct((n_rows, d), F32),
        compiler_params=_params(("arbitrary", "arbitrary"), vmem + (4 << 20)),
        name="moe_experts",
    )(block_e, n_used, xs, w_gate, w_up, w_down)


def _combine_kernel(dest_hbm, ys_hbm, x_ref, gt_ref, info_ref, o_ref, idx_smem, buf, sem_idx, sem_rows):
    i = pl.program_id(0)
    rows = x_ref.shape[0]

    def row_copy(slot, t, s):
        d = idx_smem[slot, TOP_K * t + s]
        return pltpu.make_async_copy(ys_hbm.at[pl.ds(d, 1), :], buf.at[slot, s, pl.ds(t, 1), :], sem_rows.at[slot])

    def fetch(step, slot):
        idx_copy = pltpu.make_async_copy(dest_hbm.at[step], idx_smem.at[slot], sem_idx)
        idx_copy.start()
        idx_copy.wait()

        def issue(t, carry):
            for s in range(TOP_K):
                row_copy(slot, t, s).start()
            return carry

        lax.fori_loop(0, rows, issue, 0)

    def step(slot):
        @pl.when(i + 1 < pl.num_programs(0))
        def _():
            fetch(i + 1, 1 - slot)

        def drain(t, carry):
            for s in range(TOP_K):
                row_copy(slot, t, s).wait()
            return carry

        lax.fori_loop(0, rows, drain, 0)
        info = info_ref[...]
        mix = info[:, 2:3] * buf[slot, 0] + info[:, 3:4] * buf[slot, 1]
        o_ref[...] = x_ref[...] + gt_ref[0] * mix

    @pl.when(i == 0)
    def _():
        fetch(0, 0)

    for slot in range(2):
        pl.when(i % 2 == slot)(functools.partial(step, slot))


def _combine(ys, dest, x, mod, info, tmd):
    nt, d = x.shape
    return pl.pallas_call(
        _combine_kernel,
        grid=(nt // tmd,),
        in_specs=[pl.BlockSpec(memory_space=pl.ANY),
                  pl.BlockSpec(memory_space=pl.ANY),
                  pl.BlockSpec((tmd, d), lambda i: (i, 0)),
                  mod.spec(5),
                  pl.BlockSpec((tmd, LANE), lambda i: (i, 0))],
        out_specs=pl.BlockSpec((tmd, d), lambda i: (i, 0)),
        out_shape=jax.ShapeDtypeStruct((nt, d), F32),
        scratch_shapes=[pltpu.SMEM((2, TOP_K * tmd), jnp.int32),
                        pltpu.VMEM((2, TOP_K, tmd, d), F32),
                        pltpu.SemaphoreType.DMA(()), pltpu.SemaphoreType.DMA((2,))],
        compiler_params=_params(("arbitrary",), 12 * tmd * d * 4 + (8 << 20)),
        name="moe_combine",
    )(dest.reshape(nt // tmd, TOP_K * tmd), ys, x, mod.table, info)


def _moe(x, gain, mod_route, mod_comb, w_router, b_router, w_gate, w_up, w_down, tm, tmd):
    nt, d = x.shape
    n_experts = w_router.shape[1]
    y, info, cnt = _route(x, gain, mod_route, w_router, b_router, tm)
    counts = cnt[0, :n_experts].astype(jnp.int32)
    padded = (counts + MOE_TILE - 1) // MOE_TILE * MOE_TILE
    pend = jnp.cumsum(padded)
    pstart = pend - padded
    n_rows = (-(-(nt * TOP_K) // MOE_TILE) + n_experts) * MOE_TILE
    nb = n_rows // MOE_TILE
    e_idx = info[:, 0:TOP_K].astype(jnp.int32)
    rank = info[:, 4:4 + TOP_K].astype(jnp.int32)
    dest = (pstart[e_idx] + rank).reshape(-1)
    block_e = jnp.minimum(jnp.searchsorted(pend, jnp.arange(nb, dtype=jnp.int32) * MOE_TILE, side='right'),
                          n_experts - 1).astype(jnp.int32)
    n_used = (pend[-1:] // MOE_TILE).astype(jnp.int32)
    xs = _dispatch(y, dest, pstart.astype(jnp.int32), pend.astype(jnp.int32), n_used, n_rows, tmd)
    ys = _experts(xs, block_e, n_used, w_gate, w_up, w_down, MOE_TILE)
    return _combine(ys, dest, x, mod_comb, info, tmd)


def _attn_kernel(seq_len, sink_ref, q_ref, kp_ref, km_ref, kn_ref, kc_ref, vp_ref, vm_ref, vn_ref, vc_ref, o_ref):
    i = pl.program_id(1)
    tq = q_ref.shape[0]
    n_ctx = kc_ref.shape[0]
    span = tq + 2 * WINDOW
    q_pos = i * tq + lax.broadcasted_iota(jnp.int32, (tq, span), 0)
    k_pos = i * tq - WINDOW + lax.broadcasted_iota(jnp.int32, (tq, span), 1)
    ok = (jnp.abs(q_pos - k_pos) <= WINDOW) & (k_pos >= 0) & (k_pos < seq_len)
    bias = jnp.where(ok, 0.0, -1e30).astype(F32)
    n_kv = km_ref.shape[1] // HEAD_DIM
    ones_b = jnp.ones((span, HEAD_DIM), BF16)
    ones_c = jnp.ones((n_ctx, HEAD_DIM), BF16)
    nt_dims = (((1,), (1,)), ((), ()))
    for kh in range(n_kv):
        hs = slice(kh * HEAD_DIM, (kh + 1) * HEAD_DIM)
        k_band = jnp.concatenate([kp_ref[:, hs], km_ref[:, hs], kn_ref[:, hs]], axis=0)
        k_ctx = kc_ref[:, hs]
        v_band = jnp.concatenate([jnp.concatenate([vp_ref[:, hs], vm_ref[:, hs], vn_ref[:, hs]], axis=0), ones_b],
                                 axis=1)
        v_ctx = jnp.concatenate([vc_ref[:, hs], ones_c], axis=1)
        for g in range(KV_GROUP):
            h = kh * KV_GROUP + g
            qs = slice(h * HEAD_DIM, (h + 1) * HEAD_DIM)
            s_b = lax.dot_general(q_ref[:, qs], k_band, nt_dims, preferred_element_type=F32) + bias
            s_c = lax.dot_general(q_ref[:, qs], k_ctx, nt_dims, preferred_element_type=F32)
            sink = sink_ref[h] * LOG2E
            m = jnp.maximum(jnp.maximum(jnp.max(s_b, axis=-1, keepdims=True),
                                        jnp.max(s_c, axis=-1, keepdims=True)), sink)
            p_b = jnp.exp2(s_b - m).astype(BF16)
            p_c = jnp.exp2(s_c - m).astype(BF16)
            acc = (jnp.dot(p_b, v_band, preferred_element_type=F32)
                   + jnp.dot(p_c, v_ctx, preferred_element_type=F32))
            denom = acc[:, HEAD_DIM:HEAD_DIM + 1] + jnp.exp2(sink - m)
            o_ref[:, qs] = (acc[:, :HEAD_DIM] / denom).astype(o_ref.dtype)


def _attention(qkv, sink, n_batch, seq_len, ctx_len, d):
    tq = _pow2_tile(256, seq_len)
    kvw = d // KV_GROUP
    wb = WINDOW
    qb, sb = seq_len // tq, seq_len // wb
    kcol, vcol = d // kvw, d // kvw + 1
    n_lat = n_batch * seq_len
    ctx_block0 = n_lat // ctx_len

    def prev_map(col):
        return lambda b, i, s: (b * sb + jnp.maximum(i * (tq // wb) - 1, 0), col)

    def main_map(col):
        return lambda b, i, s: (b * qb + i, col)

    def next_map(col):
        return lambda b, i, s: (b * sb + jnp.minimum((i + 1) * (tq // wb), sb - 1), col)

    def ctx_map(col):
        return lambda b, i, s: (ctx_block0 + b, col)

    kv_specs = []
    for col in (kcol, vcol):
        kv_specs += [pl.BlockSpec((wb, kvw), prev_map(col)), pl.BlockSpec((tq, kvw), main_map(col)),
                     pl.BlockSpec((wb, kvw), next_map(col)), pl.BlockSpec((ctx_len, kvw), ctx_map(col))]
    grid_spec = pltpu.PrefetchScalarGridSpec(
        num_scalar_prefetch=1,
        grid=(n_batch, qb),
        in_specs=[pl.BlockSpec((tq, d), lambda b, i, s: (b * qb + i, 0))] + kv_specs,
        out_specs=pl.BlockSpec((tq, d), lambda b, i, s: (b * qb + i, 0)),
    )
    return pl.pallas_call(
        functools.partial(_attn_kernel, seq_len),
        grid_spec=grid_spec,
        out_shape=jax.ShapeDtypeStruct((n_lat, d), BF16),
        compiler_params=_params(("arbitrary", "arbitrary"), 32 << 20),
        name="attention",
    )(sink, *([qkv] * 9))


def _rope_tables(seq_len, tm):
    rows = seq_len // GRID_W
    row = np.repeat(np.arange(rows, dtype=np.float32), GRID_W)
    col = np.tile(np.arange(GRID_W, dtype=np.float32), rows)
    axis_dim = HEAD_DIM // 2
    inv_freq = (np.float32(ROPE_BASE) ** (-np.arange(0, axis_dim, 2, dtype=np.float32) / np.float32(axis_dim)))
    ang_r = (row[:, None] * inv_freq[None, :]).astype(np.float32)
    ang_c = (col[:, None] * inv_freq[None, :]).astype(np.float32)
    cos = np.concatenate([np.cos(ang_r), np.cos(ang_c), np.cos(ang_r), np.cos(ang_c)], axis=1)
    sin = np.concatenate([-np.sin(ang_r), -np.sin(ang_c), np.sin(ang_r), np.sin(ang_c)], axis=1)
    cos = np.concatenate([cos, np.ones((tm, HEAD_DIM), np.float32)], axis=0)
    sin = np.concatenate([sin, np.zeros((tm, HEAD_DIM), np.float32)], axis=0)
    return jnp.asarray(cos, F32), jnp.asarray(sin, F32)


def _chan_dft_kernel(x_ref, g_ref, sh_ref, sc_ref, cs_ref, a_ref, b_ref, y_scr):
    _norm_mod_rows(x_ref, g_ref, sh_ref, sc_ref, y_scr, BF16)
    gw = cs_ref.shape[0]
    cs = cs_ref[...]
    for g in range(x_ref.shape[1] // gw):
        cols = slice(g * gw, (g + 1) * gw)
        r = jnp.dot(y_scr[:, cols], cs, preferred_element_type=F32)
        a_ref[:, cols] = r[:, :gw].astype(a_ref.dtype)
        b_ref[:, cols] = r[:, gw:].astype(b_ref.dtype)


def _dft_mats(n):
    k = np.arange(n, dtype=np.int64)
    ang = 2.0 * np.pi * ((k[:, None] * k[None, :]) % n).astype(np.float64) / n
    scale = 1.0 / math.sqrt(n)
    return np.cos(ang) * scale, np.sin(ang) * scale


def _chan_dft(x, gain, mod, tm):
    nt, d = x.shape
    gw = d // FNET_GROUPS
    c, s = _dft_mats(gw)
    cs = jnp.asarray(np.concatenate([c, s], axis=1), BF16)
    return pl.pallas_call(
        _chan_dft_kernel,
        grid=(nt // tm,),
        in_specs=[pl.BlockSpec((tm, d), lambda i: (i, 0)),
                  pl.BlockSpec((1, d), lambda i: (0, 0)),
                  mod.spec(0), mod.spec(1),
                  pl.BlockSpec((gw, 2 * gw), lambda i: (0, 0))],
        out_specs=[pl.BlockSpec((tm, d), lambda i: (i, 0)), pl.BlockSpec((tm, d), lambda i: (i, 0))],
        out_shape=[jax.ShapeDtypeStruct((nt, d), BF16), jax.ShapeDtypeStruct((nt, d), BF16)],
        scratch_shapes=[pltpu.VMEM((tm, d), BF16)],
        compiler_params=_params(("arbitrary",), 2 * tm * d * 4 + 5 * tm * d * 2 + (8 << 20)),
        name="fnet_channel_dft",
    )(x, gain.reshape(1, d), mod.table, mod.table, cs)


FFT_RADIX = 4


def _pos_fft_kernel(a_ref, b_ref, twc_ref, tws_ref, c_ref, s_ref, o_ref):
    q = a_ref.shape[0] // FFT_RADIX
    tn = a_ref.shape[1]
    ar = [a_ref[k * q:(k + 1) * q, :].astype(F32) for k in range(FFT_RADIX)]
    br = [b_ref[k * q:(k + 1) * q, :].astype(F32) for k in range(FFT_RADIX)]
    s02r, d02r, s13r, d13r = ar[0] + ar[2], ar[0] - ar[2], ar[1] + ar[3], ar[1] - ar[3]
    s02b, d02b, s13b, d13b = br[0] + br[2], br[0] - br[2], br[1] + br[3], br[1] - br[3]
    g = [(s02r + s13r, s02b + s13b), (d02r - d13b, d02b + d13r),
         (s02r - s13r, s02b - s13b), (d02r + d13b, d02b - d13r)]
    for j in range(FFT_RADIX):
        gr, gb = g[j]
        if j > 0:
            rows = slice(j * q, (j + 1) * q)
            tc = jnp.concatenate([twc_ref[rows, :]] * (tn // LANE), axis=1)
            ts = jnp.concatenate([tws_ref[rows, :]] * (tn // LANE), axis=1)
            gr, gb = gr * tc - gb * ts, gb * tc + gr * ts
        f = (jnp.dot(c_ref[...], gr.astype(BF16), preferred_element_type=F32)
             - jnp.dot(s_ref[...], gb.astype(BF16), preferred_element_type=F32))
        o_ref[j * q:(j + 1) * q, :] = f.astype(o_ref.dtype)


def _pos_fft(a, b, n_batch, seq_len):
    nt, d = a.shape
    q = seq_len // FFT_RADIX
    c, s = _dft_mats(q)
    scale = math.sqrt(q) / math.sqrt(seq_len)
    cq, sq = jnp.asarray(c * scale, BF16), jnp.asarray(s * scale, BF16)
    l2 = np.arange(q, dtype=np.int64)
    ang = np.concatenate([2.0 * np.pi * ((l2 * j) % seq_len) / seq_len for j in range(FFT_RADIX)])
    twc = jnp.asarray(np.broadcast_to(np.cos(ang)[:, None], (seq_len, LANE)), F32)
    tws = jnp.asarray(np.broadcast_to(np.sin(ang)[:, None], (seq_len, LANE)), F32)
    tn = _pow2_tile(256, d)
    return pl.pallas_call(
        _pos_fft_kernel,
        grid=(n_batch, d // tn),
        in_specs=[pl.BlockSpec((seq_len, tn), lambda bb, n: (bb, n)),
                  pl.BlockSpec((seq_len, tn), lambda bb, n: (bb, n)),
                  pl.BlockSpec((seq_len, LANE), lambda bb, n: (0, 0)),
                  pl.BlockSpec((seq_len, LANE), lambda bb, n: (0, 0)),
                  pl.BlockSpec((q, q), lambda bb, n: (0, 0)),
                  pl.BlockSpec((q, q), lambda bb, n: (0, 0))],
        out_specs=pl.BlockSpec((seq_len, tn), lambda bb, n: (bb, n)),
        out_shape=jax.ShapeDtypeStruct((nt, d), BF16),
        compiler_params=_params(("arbitrary", "arbitrary"),
                                6 * seq_len * tn * 2 + 4 * seq_len * LANE * 4 + 4 * q * q * 2
                                + 12 * seq_len * tn * 4 + (4 << 20)),
        name="fnet_position_fft",
    )(a, b, twc, tws, cq, sq)


def _resid_mm_interleave_kernel(f0_ref, f1_ref, f2_ref, f3_ref, p_ref, w_ref, x_ref, gt_ref, o_ref):
    fcat = jnp.concatenate([f[...] for f in (f0_ref, f1_ref, f2_ref, f3_ref)], axis=0)
    ftrue = jnp.dot(p_ref[...], fcat, preferred_element_type=F32).astype(BF16)
    o_ref[...] = x_ref[...] + gt_ref[0] * jnp.dot(ftrue, w_ref[...], preferred_element_type=F32)


def _resid_mm_interleave(f, w, x, mod, gate_chunk, seq_len, tm):
    n_rows, d = x.shape
    k = w.shape[0]
    q = seq_len // FFT_RADIX
    rj = tm // FFT_RADIX
    per_seq = seq_len // tm

    def f_map(j):
        return lambda i: ((i // per_seq) * (seq_len // rj) + j * (q // rj) + i % per_seq, 0)

    perm = np.zeros((tm, tm), np.float32)
    r = np.arange(rj)
    for j in range(FFT_RADIX):
        perm[FFT_RADIX * r + j, j * rj + r] = 1.0
    return pl.pallas_call(
        _resid_mm_interleave_kernel,
        grid=(n_rows // tm,),
        in_specs=[pl.BlockSpec((rj, k), f_map(j)) for j in range(FFT_RADIX)]
        + [pl.BlockSpec((tm, tm), lambda i: (0, 0)),
           pl.BlockSpec((k, d), lambda i: (0, 0)),
           pl.BlockSpec((tm, d), lambda i: (i, 0)),
           mod.spec(gate_chunk)],
        out_specs=pl.BlockSpec((tm, d), lambda i: (i, 0)),
        out_shape=jax.ShapeDtypeStruct((n_rows, d), F32),
        compiler_params=_params(("arbitrary",), 4 * tm * k * 2 + 2 * k * d * 2 + 6 * tm * d * 4 + (4 << 20)),
        name="fnet_out",
    )(f, f, f, f, jnp.asarray(perm, BF16), w, x, mod.table)


def kernel(x, c, ctx, c_ctx, ada_w, ada_b, norm_mix_g, norm_ffn_g, gm_w_in, gm_v_g, gm_w_s, gm_b_s, gm_w_out, cv_w_pw1, cv_b_pw1, cv_w_dw, cv_b_dw, cv_norm_g, cv_w_pw2, at_w_qkv, at_q_g, at_k_g, at_sink, at_w_o, ft_w_out, f_w_gate, f_w_up, f_w_down, m_w_router, m_b_router, m_w_gate, m_w_up, m_w_down):
    n_batch, seq_len, d = x.shape
    ctx_len = ctx.shape[1]
    depth = ada_w.shape[0]
    assert depth == 4 and seq_len % GRID_W == 0 and ctx_len % CHUNK == 0 and seq_len % CHUNK == 0
    assert d % (HEAD_DIM * KV_GROUP) == 0 and n_batch < 16
    n_lat, n_ctx = n_batch * seq_len, n_batch * ctx_len
    nt = n_lat + n_ctx
    tm = _pow2_tile(ROW_TILE, seq_len, n_ctx)
    tn = _pow2_tile(COL_TILE, d // KV_GROUP)
    tmd = _pow2_tile(DMA_ROWS, seq_len, n_ctx)
    tms = _pow2_tile(512, seq_len, n_ctx)
    bf = lambda w: w.astype(BF16)

    cc = jnp.zeros((16, d), F32).at[:n_batch].set(c).at[n_batch].set(c_ctx)
    table = _ada_table(cc, ada_w, ada_b).reshape(depth, 16 * 6, 1, d)
    mods = [_Mod(table[l], tm, seq_len, n_batch) for l in range(depth)]
    mods_s = [_Mod(table[l], tms, seq_len, n_batch) for l in range(depth)]
    mods_d = [_Mod(table[l], tmd, seq_len, n_batch) for l in range(depth)]

    x_lat, x_ctx = x.reshape(n_lat, d), ctx.reshape(n_ctx, d)

    h = _proj_call(functools.partial(_proj_gelu_kernel, n_lat // tm), (x_lat, x_ctx), norm_mix_g[0], mods[0], 0,
                   [(bf(gm_w_in[0]), 0)], [], [], 2 * d, tm, tn, "gmlp_in")
    xs = _gmlp_tail(h, gm_v_g[0], gm_w_s[0], gm_b_s[0], bf(gm_w_out[0]), x_lat, x_ctx, mods_s[0], tms)
    xs = _ffn(xs, norm_ffn_g[0], mods[0], bf(f_w_gate[0]), bf(f_w_up[0]), bf(f_w_down[0]), tm)

    w_pw1 = bf(cv_w_pw1[0])
    b_pw1 = cv_b_pw1[0].reshape(1, 2 * d)
    bias_specs = [pl.BlockSpec((1, tn), lambda i, j: (0, j)),
                  pl.BlockSpec((1, tn), lambda i, j: (0, j + d // tn))]
    z = _proj_call(_proj_glu_kernel, xs, norm_mix_g[1], mods[1], 0, [(w_pw1, 0), (w_pw1, d // tn)],
                   [b_pw1, b_pw1], bias_specs, d, tm, tn, "conv_in")
    tmc = _pow2_tile(256, seq_len, ctx_len)
    xs = _conv_tail(z, cv_w_dw[0], cv_b_dw[0], cv_norm_g[0], bf(cv_w_pw2[0]), xs,
                    _Mod(table[1], tmc, seq_len, n_batch), tmc, seq_len, n_lat, ctx_len)
    xs = _moe(xs, norm_ffn_g[1], mods[1], mods_d[1], m_w_router[0], m_b_router[0],
              bf(m_w_gate[0]), bf(m_w_up[0]), bf(m_w_down[0]), tm, tmd)

    qkv_dim = at_w_qkv.shape[2]
    n_qk_tiles = (d + d // KV_GROUP) // tn
    cos, sin = _rope_tables(seq_len, tm)
    rope_block = lambda i, j: (jnp.where(i < n_lat // tm, i % (seq_len // tm), seq_len // tm), 0)
    q_gain = _rope_layout(at_q_g[0] * (HEAD_DIM ** -0.5 * LOG2E), 1)
    k_gain = _rope_layout(at_k_g[0], 1)
    gains = jnp.stack([q_gain] * (d // tn) + [k_gain] * (qkv_dim // tn - d // tn)).reshape(-1, 1, HEAD_DIM)
    qkv_specs = [pl.BlockSpec((1, 1, HEAD_DIM), lambda i, j: (j, 0, 0)),
                 pl.BlockSpec((tm, HEAD_DIM), rope_block), pl.BlockSpec((tm, HEAD_DIM), rope_block)]
    n_qk = d + d // KV_GROUP
    w_qkv = jnp.concatenate([_rope_layout(at_w_qkv[0][:, :n_qk], n_qk // HEAD_DIM), at_w_qkv[0][:, n_qk:]], axis=1)
    qkv = _proj_call(functools.partial(_proj_qkv_kernel, n_qk_tiles), xs, norm_mix_g[2], mods[2], 0,
                     [(bf(w_qkv), 0)], [gains, cos, sin], qkv_specs, qkv_dim, tm, tn, "qkv")
    o = _attention(qkv, at_sink[0], n_batch, seq_len, ctx_len, d)
    xl = _resid_mm(o, bf(at_w_o[0]), xs, mods_s[2], 2, n_lat, tms, "attn_out")
    xl = _ffn(xl, norm_ffn_g[2], mods[2], bf(f_w_gate[1]), bf(f_w_up[1]), bf(f_w_down[1]), tm)

    a, b = _chan_dft(xl, norm_mix_g[3], mods[3], tm)
    f = _pos_fft(a, b, n_batch, seq_len)
    xl = _resid_mm_interleave(f, bf(ft_w_out[0]), xl, mods_s[3], 2, seq_len, tms)
    xl = _moe(xl, norm_ffn_g[3], mods[3], mods_d[3], m_w_router[1], m_b_router[1],
              bf(m_w_gate[1]), bf(m_w_up[1]), bf(m_w_down[1]), tm, tmd)
    return xl.reshape(n_batch, seq_len, d)
```

```python
import functools
import math

import numpy as np
import jax
import jax.numpy as jnp
from jax import lax
from jax.experimental import pallas as pl
from jax.experimental.pallas import tpu as pltpu

F32 = jnp.float32
BF16 = jnp.bfloat16

EPS = 1e-6
GRID_W = 64
CHUNK = 128
GMLP_GROUPS = 8
CONV_WIDTH = 31
CONV_PAD = CONV_WIDTH // 2
HEAD_DIM = 128
KV_GROUP = 4
WINDOW = 128
ROPE_BASE = 10000.0
FNET_GROUPS = 8
TOP_K = 2
LOG2E = math.log2(math.e)

LANE = 128
V7X_VMEM_LIMIT = 56 << 20

ROW_TILE = 1024
COL_TILE = 512
CONV_HALO = 16
MOE_TILE = 1024
DMA_ROWS = 512
CAST_BLOCK_BYTES = 8 << 20


def _params(sem, vmem_bytes):
    return pltpu.CompilerParams(dimension_semantics=sem,
                                vmem_limit_bytes=int(min(max(vmem_bytes, 16 << 20), V7X_VMEM_LIMIT)))


def _pow2_tile(target, *sizes):
    t = target
    while any(s % t for s in sizes):
        t //= 2
    return t


def _sigmoid(x):
    return 1.0 / (1.0 + jnp.exp(-x))


def _norm_mod(x, g, shift, scale):
    ms = jnp.mean(x * x, axis=-1, keepdims=True)
    return (x * lax.rsqrt(ms + EPS) * g) * (1.0 + scale) + shift


def _norm_mod_rows(x_ref, g_ref, sh_ref, sc_ref, dst_ref, dtype, chunk=256):
    rows, d = x_ref.shape
    chunk = min(chunk, rows)
    schunk = min(4 * chunk, rows)
    gs = g_ref[...] * (1.0 + sc_ref[0])
    sh = sh_ref[0]

    def run(rs_scr):
        def stats(r, carry):
            sl = pl.ds(pl.multiple_of(r * schunk, schunk), schunk)
            xv = x_ref[sl, :]
            rs_scr[sl, :] = lax.rsqrt(jnp.sum(xv * xv, axis=-1, keepdims=True) * (1.0 / d) + EPS)
            return carry

        lax.fori_loop(0, rows // schunk, stats, 0)

        def apply(r, carry):
            sl = pl.ds(pl.multiple_of(r * chunk, chunk), chunk)
            dst_ref[sl, :] = (x_ref[sl, :] * rs_scr[sl, :] * gs + sh).astype(dtype)
            return carry

        lax.fori_loop(0, rows // chunk, apply, 0)

    pl.run_scoped(run, pltpu.VMEM((rows, 1), F32))


def _cast_kernel(x_ref, o_ref):
    o_ref[...] = x_ref[0].astype(o_ref.dtype)


def _cast_layer_bf16(w, layer):
    k, n = w.shape[-2:]
    rk = _pow2_tile(1 << int(math.log2(max(8, CAST_BLOCK_BYTES // (n * 4)))), k)
    if w.ndim == 4:
        grid = (w.shape[1], k // rk)
        in_spec = pl.BlockSpec((1, 1, rk, n), lambda e, r: (layer, e, r, 0))
        out_spec = pl.BlockSpec((1, rk, n), lambda e, r: (e, r, 0))
    else:
        grid = (k // rk,)
        in_spec = pl.BlockSpec((1, rk, n), lambda r: (layer, r, 0))
        out_spec = pl.BlockSpec((rk, n), lambda r: (r, 0))
    return pl.pallas_call(
        _cast_kernel,
        grid=grid,
        in_specs=[in_spec],
        out_specs=out_spec,
        out_shape=jax.ShapeDtypeStruct(w.shape[1:], BF16),
        compiler_params=_params(("arbitrary",) * len(grid), 4 * rk * n * 4 + (4 << 20)),
        name="cast_bf16",
    )(w)


def _ada_kernel(c_ref, w_ref, b_ref, o_ref):
    c = c_ref[...]
    s = (c * _sigmoid(c)).astype(BF16)
    o_ref[0] = jnp.dot(s, w_ref[0].astype(BF16), preferred_element_type=F32) + b_ref[0]


def _ada_table(cc, ada_w, ada_b):
    depth, d, n6 = ada_w.shape
    tn = _pow2_tile(1024, n6)
    rows = cc.shape[0]
    return pl.pallas_call(
        _ada_kernel,
        grid=(depth, n6 // tn),
        in_specs=[pl.BlockSpec((rows, d), lambda l, j: (0, 0)),
                  pl.BlockSpec((1, d, tn), lambda l, j: (l, 0, j)),
                  pl.BlockSpec((1, 1, tn), lambda l, j: (l, 0, j))],
        out_specs=pl.BlockSpec((1, rows, tn), lambda l, j: (l, 0, j)),
        out_shape=jax.ShapeDtypeStruct((depth, rows, n6), F32),
        compiler_params=_params(("arbitrary", "arbitrary"), 2 * d * tn * 4 + d * tn * 2 + (8 << 20)),
        name="ada_table",
    )(cc, ada_w, ada_b.reshape(depth, 1, n6))


class _Mod:
    def __init__(self, table, tm, seq_len, n_batch):
        self.table, self.tm, self.seq_len, self.n_batch = table, tm, seq_len, n_batch
        self.d = table.shape[-1]

    def spec(self, chunk):
        tm, seq_len, n_batch = self.tm, self.seq_len, self.n_batch

        def index(i, *_):
            return (jnp.minimum(i * tm // seq_len, n_batch) * 6 + chunk, 0, 0)

        return pl.BlockSpec((1, 1, self.d), index)


def _gelu_tanh(x):
    return 0.5 * x * (1.0 + jnp.tanh(math.sqrt(2.0 / math.pi) * (x + 0.044715 * (x * x * x))))


def _proj_gelu_kernel(n_lat_blocks, xl_ref, xc_ref, g_ref, sh_ref, sc_ref, w_ref, o_ref, y_scr):
    i, j = pl.program_id(0), pl.program_id(1)

    @pl.when(jnp.logical_and(j == 0, i < n_lat_blocks))
    def _():
        _norm_mod_rows(xl_ref, g_ref, sh_ref, sc_ref, y_scr, BF16)

    @pl.when(jnp.logical_and(j == 0, i >= n_lat_blocks))
    def _():
        _norm_mod_rows(xc_ref, g_ref, sh_ref, sc_ref, y_scr, BF16)

    acc = jnp.dot(y_scr[...], w_ref[j], preferred_element_type=F32)
    o_ref[...] = _gelu_tanh(acc).astype(o_ref.dtype)


def _proj_glu_kernel(x_ref, g_ref, sh_ref, sc_ref, w_ref, ba_ref, bg_ref, o_ref, y_scr):
    j = pl.program_id(1)

    @pl.when(j == 0)
    def _():
        _norm_mod_rows(x_ref, g_ref, sh_ref, sc_ref, y_scr, BF16)

    y = y_scr[...]
    half = w_ref.shape[0] // 2
    a = jnp.dot(y, w_ref[j], preferred_element_type=F32) + ba_ref[...]
    gate = jnp.dot(y, w_ref[j + half], preferred_element_type=F32) + bg_ref[...]
    o_ref[...] = (a * _sigmoid(gate)).astype(o_ref.dtype)


def _rope_layout(t, n_heads):
    lead = t.shape[:-1]
    q = HEAD_DIM // 4
    t = t.reshape(*lead, n_heads, 2, 2, q)
    return jnp.swapaxes(t, -2, -3).reshape(*lead, n_heads * HEAD_DIM)


def _rope_partner(x):
    return pltpu.roll(x, HEAD_DIM // 2, 1)


def _proj_qkv_kernel(n_qk_tiles, x_ref, g_ref, sh_ref, sc_ref, w_ref, hg_ref, cos_ref, sin_ref, o_ref, y_scr):
    j = pl.program_id(1)

    @pl.when(j == 0)
    def _():
        _norm_mod_rows(x_ref, g_ref, sh_ref, sc_ref, y_scr, BF16)

    acc = jnp.dot(y_scr[...], w_ref[j], preferred_element_type=F32)

    @pl.when(j < n_qk_tiles)
    def _():
        hg, cos, sin = hg_ref[0], cos_ref[...], sin_ref[...]
        for h in range(acc.shape[1] // HEAD_DIM):
            t = acc[:, h * HEAD_DIM:(h + 1) * HEAD_DIM]
            t = t * lax.rsqrt(jnp.mean(t * t, axis=-1, keepdims=True) + EPS) * hg
            t = t * cos + _rope_partner(t) * sin
            o_ref[:, h * HEAD_DIM:(h + 1) * HEAD_DIM] = t.astype(o_ref.dtype)

    @pl.when(j >= n_qk_tiles)
    def _():
        o_ref[...] = acc.astype(o_ref.dtype)


def _proj_call(kernel, x, gain, mod, chunk0, w, extra_in, extra_specs, n_out, tm, tn, name):
    if isinstance(x, tuple):
        x_lat, x_ctx = x
        nlb = x_lat.shape[0] // tm
        nt, d = x_lat.shape[0] + x_ctx.shape[0], x_lat.shape[1]
        in_specs = [pl.BlockSpec((tm, d), lambda i, j: (jnp.minimum(i, nlb - 1), 0)),
                    pl.BlockSpec((tm, d), lambda i, j: (jnp.maximum(i - nlb, 0), 0))]
        args = [x_lat, x_ctx]
    else:
        nt, d = x.shape
        in_specs = [pl.BlockSpec((tm, d), lambda i, j: (i, 0))]
        args = [x]
    in_specs += [pl.BlockSpec((1, d), lambda i, j: (0, 0)), mod.spec(chunk0), mod.spec(chunk0 + 1)]
    args += [gain.reshape(1, d), mod.table, mod.table]
    n_x = len(args)
    n_w = w.shape[1]
    w_tiles = jnp.swapaxes(w.reshape(d, n_w // tn, tn), 0, 1)
    in_specs.append(pl.BlockSpec((n_w // tn, d, tn), lambda i, j: (0, 0, 0), pipeline_mode=pl.Buffered(1)))
    args.append(w_tiles)
    in_specs += extra_specs
    args += extra_in
    vmem = 2 * n_x * tm * d * 4 + tm * d * 2 + d * n_w * 2 + 2 * tm * tn * 2 + 6 * tm * tn * 4
    return pl.pallas_call(
        kernel,
        grid=(nt // tm, n_out // tn),
        in_specs=in_specs,
        out_specs=pl.BlockSpec((tm, tn), lambda i, j: (i, j)),
        out_shape=jax.ShapeDtypeStruct((nt, n_out), BF16),
        scratch_shapes=[pltpu.VMEM((tm, d), BF16)],
        compiler_params=_params(("arbitrary", "arbitrary"), vmem + (4 << 20)),
        name=name,
    )(*args)


def _resid_mm_kernel(a_ref, w_ref, x_ref, gt_ref, o_ref):
    o_ref[...] = x_ref[...] + gt_ref[0] * jnp.dot(a_ref[...], w_ref[...], preferred_element_type=F32)


def _resid_mm(a, w, x, mod, gate_chunk, n_rows, tm, name):
    k, d = w.shape
    return pl.pallas_call(
        _resid_mm_kernel,
        grid=(n_rows // tm,),
        in_specs=[pl.BlockSpec((tm, k), lambda i: (i, 0)),
                  pl.BlockSpec((k, d), lambda i: (0, 0)),
                  pl.BlockSpec((tm, d), lambda i: (i, 0)),
                  mod.spec(gate_chunk)],
        out_specs=pl.BlockSpec((tm, d), lambda i: (i, 0)),
        out_shape=jax.ShapeDtypeStruct((n_rows, d), F32),
        compiler_params=_params(("arbitrary",), 2 * tm * k * 2 + 2 * k * d * 2 + 5 * tm * d * 4 + (4 << 20)),
        name=name,
    )(a, w, x, mod.table)


def _gmlp_tail_kernel(n_lat_blocks, h_ref, vg_ref, ws_ref, bs_ref, w_ref, xl_ref, xc_ref, gt_ref, o_ref, z_scr):
    tm, d = xl_ref.shape
    gw = d // GMLP_GROUPS
    v = h_ref[:, d:].astype(F32)
    vn = (v * lax.rsqrt(jnp.mean(v * v, axis=-1, keepdims=True) + EPS) * vg_ref[...]).astype(BF16)
    for c in range(tm // CHUNK):
        rows = slice(c * CHUNK, (c + 1) * CHUNK)
        for g in range(GMLP_GROUPS):
            cols = slice(g * gw, (g + 1) * gw)
            sv = jnp.dot(ws_ref[g], vn[rows, cols], preferred_element_type=F32) + bs_ref[g]
            z_scr[rows, cols] = (h_ref[rows, cols].astype(F32) * sv).astype(BF16)
    x = jnp.where(pl.program_id(0) < n_lat_blocks, xl_ref[...], xc_ref[...])
    o_ref[...] = x + gt_ref[0] * jnp.dot(z_scr[...], w_ref[...], preferred_element_type=F32)


def _gmlp_tail(h, v_g, w_s, b_s, w_out, x_lat, x_ctx, mod, tm):
    d = x_lat.shape[1]
    nt = x_lat.shape[0] + x_ctx.shape[0]
    nlb = x_lat.shape[0] // tm
    bs_tile = jnp.broadcast_to(b_s[:, :, None], (GMLP_GROUPS, CHUNK, d // GMLP_GROUPS)).astype(F32)
    return pl.pallas_call(
        functools.partial(_gmlp_tail_kernel, nlb),
        grid=(nt // tm,),
        in_specs=[pl.BlockSpec((tm, 2 * d), lambda i: (i, 0)),
                  pl.BlockSpec((1, d), lambda i: (0, 0)),
                  pl.BlockSpec((GMLP_GROUPS, CHUNK, CHUNK), lambda i: (0, 0, 0)),
                  pl.BlockSpec((GMLP_GROUPS, CHUNK, d // GMLP_GROUPS), lambda i: (0, 0, 0)),
                  pl.BlockSpec((d, d), lambda i: (0, 0)),
                  pl.BlockSpec((tm, d), lambda i: (jnp.minimum(i, nlb - 1), 0)),
                  pl.BlockSpec((tm, d), lambda i: (jnp.maximum(i - nlb, 0), 0)),
                  mod.spec(2)],
        out_specs=pl.BlockSpec((tm, d), lambda i: (i, 0)),
        out_shape=jax.ShapeDtypeStruct((nt, d), F32),
        scratch_shapes=[pltpu.VMEM((tm, d), BF16)],
        compiler_params=_params(("arbitrary",), 2 * tm * 2 * d * 2 + 2 * d * d * 2 + 8 * tm * d * 4 + (6 << 20)),
        name="gmlp_tail",
    )(h, v_g.reshape(1, d), w_s.astype(BF16), bs_tile, w_out, x_lat, x_ctx, mod.table)


def _conv_tail_kernel(edges, z_ref, zp_ref, zn_ref, wdw_ref, bdw_ref, ng_ref, w_ref, x_ref, gt_ref,
                      o_ref, ext_scr, cv_scr, sh_scr):
    i = pl.program_id(0)
    tm, d = x_ref.shape
    first, last = edges(i)
    ext_scr[0:CONV_HALO, :] = jnp.where(first, 0.0, zp_ref[...].astype(F32))
    ext_scr[CONV_HALO:CONV_HALO + tm, :] = z_ref[...].astype(F32)
    ext_scr[CONV_HALO + tm:, :] = jnp.where(last, 0.0, zn_ref[...].astype(F32))

    rc = min(64, tm)
    cc = sh_scr.shape[2]
    sh_rows = sh_scr.shape[1]

    def col_body(c, carry):
        cols = pl.ds(pl.multiple_of(c * cc, cc), cc)
        wts = wdw_ref[:, cols]
        bias = bdw_ref[:, cols]
        for b in range(1, 8):
            sh_scr[b] = ext_scr[b:b + sh_rows, cols]
        for r in range(tm // rc):
            acc = jnp.zeros((rc, cc), F32) + bias
            for k in range(CONV_WIDTH):
                off = r * rc + CONV_HALO - CONV_PAD + k
                b, a = off % 8, off - off % 8
                src = ext_scr[a:a + rc, cols] if b == 0 else sh_scr[b, a:a + rc, :]
                acc = acc + wts[k:k + 1, :] * src
            cv_scr[r * rc:(r + 1) * rc, cols] = acc
        return carry

    lax.fori_loop(0, d // cc, col_body, 0)
    cv = cv_scr[...]
    t = cv * lax.rsqrt(jnp.mean(cv * cv, axis=-1, keepdims=True) + EPS) * ng_ref[...]
    t = (t * _sigmoid(t)).astype(BF16)
    o_ref[...] = x_ref[...] + gt_ref[0] * jnp.dot(t, w_ref[...], preferred_element_type=F32)


def _conv_tail(z, w_dw, b_dw, n_g, w_pw2, x, mod, tm, seq_len, n_lat, ctx_len):
    nt, d = x.shape
    hb = tm // CONV_HALO
    n_halo_blocks = nt // CONV_HALO
    lat_blocks, seq_blocks, ctx_blocks = n_lat // tm, seq_len // tm, ctx_len // tm

    def edges(i):
        in_lat = i < lat_blocks
        pos = jnp.where(in_lat, i % seq_blocks, (i - lat_blocks) % ctx_blocks)
        per = jnp.where(in_lat, seq_blocks, ctx_blocks)
        return pos == 0, pos == per - 1

    kdw = w_dw.shape[0]
    kpad = -(-kdw // 8) * 8
    w_dw_p = jnp.zeros((kpad, d), F32).at[:kdw].set(w_dw)
    return pl.pallas_call(
        functools.partial(_conv_tail_kernel, edges),
        grid=(nt // tm,),
        in_specs=[pl.BlockSpec((tm, d), lambda i: (i, 0)),
                  pl.BlockSpec((CONV_HALO, d), lambda i: (jnp.maximum(i * hb - 1, 0), 0)),
                  pl.BlockSpec((CONV_HALO, d), lambda i: (jnp.minimum((i + 1) * hb, n_halo_blocks - 1), 0)),
                  pl.BlockSpec((kpad, d), lambda i: (0, 0)),
                  pl.BlockSpec((1, d), lambda i: (0, 0)),
                  pl.BlockSpec((1, d), lambda i: (0, 0)),
                  pl.BlockSpec((d, d), lambda i: (0, 0)),
                  pl.BlockSpec((tm, d), lambda i: (i, 0)),
                  mod.spec(2)],
        out_specs=pl.BlockSpec((tm, d), lambda i: (i, 0)),
        out_shape=jax.ShapeDtypeStruct((nt, d), F32),
        scratch_shapes=[pltpu.VMEM((tm + 2 * CONV_HALO, d), F32), pltpu.VMEM((tm, d), F32),
                        pltpu.VMEM((8, tm + 2 * CONV_HALO - 8, min(256, d)), F32)],
        compiler_params=_params(("arbitrary",), 2 * d * d * 2 + 12 * tm * d * 4 + (6 << 20)),
        name="conv_tail",
    )(z, z, z, w_dw_p, b_dw.reshape(1, d), n_g.reshape(1, d), w_pw2, x, mod.table)


def _copy_rows(src_ref, dst_ref, chunk=256):
    rows = src_ref.shape[0]
    chunk = min(chunk, rows)

    def body(r, carry):
        sl = pl.ds(pl.multiple_of(r * chunk, chunk), chunk)
        dst_ref[sl, :] = src_ref[sl, :].astype(dst_ref.dtype)
        return carry

    lax.fori_loop(0, rows // chunk, body, 0)


def _swiglu_partial(y, wg, wu, wd):
    hg = jnp.dot(y, wg, preferred_element_type=F32)
    hu = jnp.dot(y, wu, preferred_element_type=F32)
    h = (hg * _sigmoid(hg) * hu).astype(BF16)
    return jnp.dot(h, wd, preferred_element_type=F32)


def _ffn_kernel(x_ref, g_ref, sh_ref, sc_ref, gt_ref, wg_ref, wu_ref, wd_ref, o_ref, y_scr):
    f = pl.program_id(1)

    @pl.when(f == 0)
    def _():
        _norm_mod_rows(x_ref, g_ref, sh_ref, sc_ref, y_scr, BF16)
        o_ref[...] = x_ref[...] + gt_ref[0] * _swiglu_partial(y_scr[...], wg_ref[...], wu_ref[...], wd_ref[...])

    @pl.when(f > 0)
    def _():
        o_ref[...] += gt_ref[0] * _swiglu_partial(y_scr[...], wg_ref[...], wu_ref[...], wd_ref[...])


def _ffn(x, gain, mod, w_gate, w_up, w_down, tm):
    nt, d = x.shape
    dff = w_gate.shape[1]
    tf = 512 if dff % 512 == 0 else 256
    while dff % tf:
        tf //= 2
    vmem = 4 * tm * d * 4 + tm * d * 2 + 6 * d * tf * 2 + 5 * tm * tf * 4
    return pl.pallas_call(
        _ffn_kernel,
        grid=(nt // tm, dff // tf),
        in_specs=[pl.BlockSpec((tm, d), lambda i, f: (i, 0)),
                  pl.BlockSpec((1, d), lambda i, f: (0, 0)),
                  mod.spec(3), mod.spec(4), mod.spec(5),
                  pl.BlockSpec((d, tf), lambda i, f: (0, f)),
                  pl.BlockSpec((d, tf), lambda i, f: (0, f)),
                  pl.BlockSpec((tf, d), lambda i, f: (f, 0))],
        out_specs=pl.BlockSpec((tm, d), lambda i, f: (i, 0)),
        out_shape=jax.ShapeDtypeStruct((nt, d), F32),
        scratch_shapes=[pltpu.VMEM((tm, d), BF16)],
        compiler_params=_params(("arbitrary", "arbitrary"), vmem + (4 << 20)),
        name="ffn_dense",
    )(x, gain.reshape(1, d), mod.table, mod.table, mod.table, w_gate, w_up, w_down)


def _route_kernel(n_experts, x_ref, g_ref, sh_ref, sc_ref, wh_ref, wl_ref, br_ref, tri_ref,
                  y_ref, info_ref, cnt_ref, carry):
    i = pl.program_id(0)

    @pl.when(i == 0)
    def _():
        carry[...] = jnp.zeros_like(carry)

    _norm_mod_rows(x_ref, g_ref, sh_ref, sc_ref, y_ref, F32)
    y = y_ref[...]
    yh = y.astype(BF16)
    yl = (y - yh.astype(F32)).astype(BF16)
    wh, wl = wh_ref[...], wl_ref[...]
    lg = (jnp.dot(yh, wh, preferred_element_type=F32) + jnp.dot(yh, wl, preferred_element_type=F32)
          + jnp.dot(yl, wh, preferred_element_type=F32) + br_ref[...])
    lane = lax.broadcasted_iota(jnp.int32, lg.shape, 1)
    neg = jnp.float32(-jnp.inf)
    lg = jnp.where(lane < n_experts, lg, neg)
    m1 = jnp.max(lg, axis=-1, keepdims=True)
    i1 = jnp.min(jnp.where(lg == m1, lane, LANE), axis=-1, keepdims=True)
    lg2 = jnp.where(lane == i1, neg, lg)
    m2 = jnp.max(lg2, axis=-1, keepdims=True)
    i2 = jnp.min(jnp.where(lg2 == m2, lane, LANE), axis=-1, keepdims=True)
    e = jnp.exp(m2 - m1)
    w1 = 1.0 / (1.0 + e)
    w2 = e * w1
    hit1, hit2 = lane == i1, lane == i2
    onehot = (hit1 | hit2).astype(F32)
    pre = jnp.dot(tri_ref[...], onehot.astype(BF16), preferred_element_type=F32) + carry[0:1, :]
    r1 = jnp.sum(jnp.where(hit1, pre, 0.0), axis=-1, keepdims=True)
    r2 = jnp.sum(jnp.where(hit2, pre, 0.0), axis=-1, keepdims=True)
    total = carry[0:1, :] + jnp.sum(onehot, axis=0, keepdims=True)
    carry[...] = jnp.broadcast_to(total, carry.shape)
    cnt_ref[...] = jnp.broadcast_to(total, cnt_ref.shape)
    info = jnp.where(lane == 0, i1.astype(F32), 0.0)
    info = jnp.where(lane == 1, i2.astype(F32), info)
    info = jnp.where(lane == 2, w1, info)
    info = jnp.where(lane == 3, w2, info)
    info = jnp.where(lane == 4, r1, info)
    info = jnp.where(lane == 5, r2, info)
    info_ref[...] = info


def _route(x, gain, mod, w_router, b_router, tm):
    nt, d = x.shape
    n_experts = w_router.shape[1]
    wr = jnp.zeros((d, LANE), F32).at[:, :n_experts].set(w_router)
    wr_hi = wr.astype(BF16)
    wr_lo = (wr - wr_hi.astype(F32)).astype(BF16)
    br = jnp.zeros((1, LANE), F32).at[0, :n_experts].set(b_router)
    tri = jnp.tril(jnp.ones((tm, tm), BF16), -1)
    return pl.pallas_call(
        functools.partial(_route_kernel, n_experts),
        grid=(nt // tm,),
        in_specs=[pl.BlockSpec((tm, d), lambda i: (i, 0)),
                  pl.BlockSpec((1, d), lambda i: (0, 0)),
                  mod.spec(3), mod.spec(4),
                  pl.BlockSpec((d, LANE), lambda i: (0, 0)),
                  pl.BlockSpec((d, LANE), lambda i: (0, 0)),
                  pl.BlockSpec((1, LANE), lambda i: (0, 0)),
                  pl.BlockSpec((tm, tm), lambda i: (0, 0))],
        out_specs=[pl.BlockSpec((tm, d), lambda i: (i, 0)),
                   pl.BlockSpec((tm, LANE), lambda i: (i, 0)),
                   pl.BlockSpec((8, LANE), lambda i: (0, 0))],
        out_shape=[jax.ShapeDtypeStruct((nt, d), F32),
                   jax.ShapeDtypeStruct((nt, LANE), F32),
                   jax.ShapeDtypeStruct((8, LANE), F32)],
        scratch_shapes=[pltpu.VMEM((8, LANE), F32)],
        compiler_params=_params(("arbitrary",), 4 * tm * d * 4 + 2 * tm * d * 2 + 2 * tm * tm * 2 + (8 << 20)),
        name="moe_route",
    )(x, gain.reshape(1, d), mod.table, mod.table, wr_hi, wr_lo, br, tri)


def _expert_kernel(n_f, blk_ref, nused_ref, rowtok_ref, y_hbm, wg_ref, wu_ref, wd_ref, o_ref, xbuf, y_scr, sems):
    del blk_ref
    i, f = pl.program_id(0), pl.program_id(1)
    tm = o_ref.shape[0]
    chunk = tm // n_f
    n_used = nused_ref[0]
    used = i < n_used
    slot = i % 2
    nxt = jnp.minimum(i + 1, n_used - 1)

    def row_copy(blk, slot_, row):
        tok = rowtok_ref[blk * tm + row]
        return pltpu.make_async_copy(y_hbm.at[pl.ds(tok, 1), :], xbuf.at[slot_, pl.ds(row, 1), :], sems.at[slot_])

    def copy_rows(blk, slot_, first_row, n, start):
        def body(r, carry):
            cp = row_copy(blk, slot_, first_row + r)
            cp.start() if start else cp.wait()
            return carry

        lax.fori_loop(0, n, body, 0)

    @pl.when(jnp.logical_and(i == 0, f == 0))
    def _():
        copy_rows(0, 0, 0, tm, True)
        copy_rows(0, 0, 0, tm, False)

    def start_chunk():
        for r in range(chunk):
            row_copy(nxt, 1 - slot, f * chunk + r).start()

    @pl.when(jnp.logical_and(used, f == 0))
    def _():
        @pl.when(i > 0)
        def _():
            copy_rows(i, slot, (n_f - 1) * chunk, chunk, False)

        _copy_rows(xbuf.at[slot], y_scr)
        o_ref[...] = _swiglu_partial(y_scr[...], wg_ref[0], wu_ref[0], wd_ref[0])
        start_chunk()

    @pl.when(jnp.logical_and(used, f > 0))
    def _():
        for r in range(chunk):
            row_copy(nxt, 1 - slot, (f - 1) * chunk + r).wait()
        o_ref[...] += _swiglu_partial(y_scr[...], wg_ref[0], wu_ref[0], wd_ref[0])
        start_chunk()

    @pl.when(jnp.logical_and(i == n_used - 1, f == n_f - 1))
    def _():
        copy_rows(nxt, 1 - slot, (n_f - 1) * chunk, chunk, False)

    @pl.when(jnp.logical_and(jnp.logical_not(used), f == 0))
    def _():
        o_ref[...] = jnp.zeros_like(o_ref)


def _experts(y, row_tok, block_e, n_used, w_gate, w_up, w_down, tm):
    n_rows = row_tok.shape[0]
    d = y.shape[1]
    dff = w_gate.shape[2]
    tf = _pow2_tile(512, dff)
    n_f = dff // tf
    assert tm % n_f == 0
    vmem = 4 * tm * d * 4 + tm * d * 2 + 6 * d * tf * 2 + 5 * tm * tf * 4
    grid_spec = pltpu.PrefetchScalarGridSpec(
        num_scalar_prefetch=3,
        grid=(n_rows // tm, n_f),
        in_specs=[pl.BlockSpec(memory_space=pl.ANY),
                  pl.BlockSpec((1, d, tf), lambda i, f, be, nu, rt: (be[i], 0, f)),
                  pl.BlockSpec((1, d, tf), lambda i, f, be, nu, rt: (be[i], 0, f)),
                  pl.BlockSpec((1, tf, d), lambda i, f, be, nu, rt: (be[i], f, 0))],
        out_specs=pl.BlockSpec((tm, d), lambda i, f, be, nu, rt: (i, 0)),
        scratch_shapes=[pltpu.VMEM((2, tm, d), F32), pltpu.VMEM((tm, d), BF16), pltpu.SemaphoreType.DMA((2,))],
    )
    return pl.pallas_call(
        functools.partial(_expert_kernel, n_f),
        grid_spec=grid_spec,
        out_shape=jax.ShapeDtypeStruct((n_rows, d), F32),
        compiler_params=_params(("arbitrary", "arbitrary"), vmem + (4 << 20)),
        name="moe_experts",
    )(block_e, n_used, row_tok, y, w_gate, w_up, w_down)


def _combine_kernel(dest_hbm, ys_hbm, x_ref, gt_ref, info_ref, o_ref, idx_smem, buf, sem_idx, sem_rows):
    i = pl.program_id(0)
    rows = x_ref.shape[0]

    def row_copy(slot, t, s):
        d = idx_smem[slot, TOP_K * t + s]
        return pltpu.make_async_copy(ys_hbm.at[pl.ds(d, 1), :], buf.at[slot, s, pl.ds(t, 1), :], sem_rows.at[slot])

    def fetch(step, slot):
        idx_copy = pltpu.make_async_copy(dest_hbm.at[step], idx_smem.at[slot], sem_idx)
        idx_copy.start()
        idx_copy.wait()

        def issue(t, carry):
            for s in range(TOP_K):
                row_copy(slot, t, s).start()
            return carry

        lax.fori_loop(0, rows, issue, 0)

    def step(slot):
        @pl.when(i + 1 < pl.num_programs(0))
        def _():
            fetch(i + 1, 1 - slot)

        def drain(t, carry):
            for s in range(TOP_K):
                row_copy(slot, t, s).wait()
            return carry

        lax.fori_loop(0, rows, drain, 0)
        info = info_ref[...]
        mix = info[:, 2:3] * buf[slot, 0] + info[:, 3:4] * buf[slot, 1]
        o_ref[...] = x_ref[...] + gt_ref[0] * mix

    @pl.when(i == 0)
    def _():
        fetch(0, 0)

    for slot in range(2):
        pl.when(i % 2 == slot)(functools.partial(step, slot))


def _combine(ys, dest, x, mod, info, tmd):
    nt, d = x.shape
    return pl.pallas_call(
        _combine_kernel,
        grid=(nt // tmd,),
        in_specs=[pl.BlockSpec(memory_space=pl.ANY),
                  pl.BlockSpec(memory_space=pl.ANY),
                  pl.BlockSpec((tmd, d), lambda i: (i, 0)),
                  mod.spec(5),
                  pl.BlockSpec((tmd, LANE), lambda i: (i, 0))],
        out_specs=pl.BlockSpec((tmd, d), lambda i: (i, 0)),
        out_shape=jax.ShapeDtypeStruct((nt, d), F32),
        scratch_shapes=[pltpu.SMEM((2, TOP_K * tmd), jnp.int32),
                        pltpu.VMEM((2, TOP_K, tmd, d), F32),
                        pltpu.SemaphoreType.DMA(()), pltpu.SemaphoreType.DMA((2,))],
        compiler_params=_params(("arbitrary",), 12 * tmd * d * 4 + (8 << 20)),
        name="moe_combine",
    )(dest.reshape(nt // tmd, TOP_K * tmd), ys, x, mod.table, info)


def _moe(x, gain, mod_route, mod_comb, w_router, b_router, w_gate, w_up, w_down, tm, tmd):
    nt, d = x.shape
    n_experts = w_router.shape[1]
    y, info, cnt = _route(x, gain, mod_route, w_router, b_router, tm)
    counts = cnt[0, :n_experts].astype(jnp.int32)
    padded = (counts + MOE_TILE - 1) // MOE_TILE * MOE_TILE
    pend = jnp.cumsum(padded)
    pstart = pend - padded
    n_rows = (-(-(nt * TOP_K) // MOE_TILE) + n_experts) * MOE_TILE
    nb = n_rows // MOE_TILE
    e_idx = info[:, 0:TOP_K].astype(jnp.int32)
    rank = info[:, 4:4 + TOP_K].astype(jnp.int32)
    dest = (pstart[e_idx] + rank).reshape(-1)
    block_e = jnp.minimum(jnp.searchsorted(pend, jnp.arange(nb, dtype=jnp.int32) * MOE_TILE, side='right'),
                          n_experts - 1).astype(jnp.int32)
    n_used = (pend[-1:] // MOE_TILE).astype(jnp.int32)
    tok = jnp.repeat(jnp.arange(nt, dtype=jnp.int32), TOP_K)
    row_tok = jnp.zeros((n_rows,), jnp.int32).at[dest].set(tok, unique_indices=True)
    ys = _experts(y, row_tok, block_e, n_used, w_gate, w_up, w_down, MOE_TILE)
    return _combine(ys, dest, x, mod_comb, info, tmd)


def _attn_kernel(seq_len, sink_ref, q_ref, kp_ref, km_ref, kn_ref, kc_ref, vp_ref, vm_ref, vn_ref, vc_ref, o_ref):
    i = pl.program_id(1)
    tq = q_ref.shape[0]
    n_ctx = kc_ref.shape[0]
    span = tq + 2 * WINDOW
    q_pos = i * tq + lax.broadcasted_iota(jnp.int32, (tq, span), 0)
    k_pos = i * tq - WINDOW + lax.broadcasted_iota(jnp.int32, (tq, span), 1)
    ok = (jnp.abs(q_pos - k_pos) <= WINDOW) & (k_pos >= 0) & (k_pos < seq_len)
    bias = jnp.where(ok, 0.0, -1e30).astype(F32)
    n_kv = km_ref.shape[1] // HEAD_DIM
    ones_b = jnp.ones((span, HEAD_DIM), BF16)
    ones_c = jnp.ones((n_ctx, HEAD_DIM), BF16)
    nt_dims = (((1,), (1,)), ((), ()))
    for kh in range(n_kv):
        hs = slice(kh * HEAD_DIM, (kh + 1) * HEAD_DIM)
        k_band = jnp.concatenate([kp_ref[:, hs], km_ref[:, hs], kn_ref[:, hs]], axis=0)
        k_ctx = kc_ref[:, hs]
        v_band = jnp.concatenate([jnp.concatenate([vp_ref[:, hs], vm_ref[:, hs], vn_ref[:, hs]], axis=0), ones_b],
                                 axis=1)
        v_ctx = jnp.concatenate([vc_ref[:, hs], ones_c], axis=1)
        for g in range(KV_GROUP):
            h = kh * KV_GROUP + g
            qs = slice(h * HEAD_DIM, (h + 1) * HEAD_DIM)
            s_b = lax.dot_general(q_ref[:, qs], k_band, nt_dims, preferred_element_type=F32) + bias
            s_c = lax.dot_general(q_ref[:, qs], k_ctx, nt_dims, preferred_element_type=F32)
            sink = sink_ref[h] * LOG2E
            m = jnp.maximum(jnp.maximum(jnp.max(s_b, axis=-1, keepdims=True),
                                        jnp.max(s_c, axis=-1, keepdims=True)), sink)
            p_b = jnp.exp2(s_b - m).astype(BF16)
            p_c = jnp.exp2(s_c - m).astype(BF16)
            acc = (jnp.dot(p_b, v_band, preferred_element_type=F32)
                   + jnp.dot(p_c, v_ctx, preferred_element_type=F32))
            denom = acc[:, HEAD_DIM:HEAD_DIM + 1] + jnp.exp2(sink - m)
            o_ref[:, qs] = (acc[:, :HEAD_DIM] / denom).astype(o_ref.dtype)


def _attention(qkv, sink, n_batch, seq_len, ctx_len, d):
    tq = _pow2_tile(256, seq_len)
    kvw = d // KV_GROUP
    wb = WINDOW
    qb, sb = seq_len // tq, seq_len // wb
    kcol, vcol = d // kvw, d // kvw + 1
    n_lat = n_batch * seq_len
    ctx_block0 = n_lat // ctx_len

    def prev_map(col):
        return lambda b, i, s: (b * sb + jnp.maximum(i * (tq // wb) - 1, 0), col)

    def main_map(col):
        return lambda b, i, s: (b * qb + i, col)

    def next_map(col):
        return lambda b, i, s: (b * sb + jnp.minimum((i + 1) * (tq // wb), sb - 1), col)

    def ctx_map(col):
        return lambda b, i, s: (ctx_block0 + b, col)

    kv_specs = []
    for col in (kcol, vcol):
        kv_specs += [pl.BlockSpec((wb, kvw), prev_map(col)), pl.BlockSpec((tq, kvw), main_map(col)),
                     pl.BlockSpec((wb, kvw), next_map(col)), pl.BlockSpec((ctx_len, kvw), ctx_map(col))]
    grid_spec = pltpu.PrefetchScalarGridSpec(
        num_scalar_prefetch=1,
        grid=(n_batch, qb),
        in_specs=[pl.BlockSpec((tq, d), lambda b, i, s: (b * qb + i, 0))] + kv_specs,
        out_specs=pl.BlockSpec((tq, d), lambda b, i, s: (b * qb + i, 0)),
    )
    return pl.pallas_call(
        functools.partial(_attn_kernel, seq_len),
        grid_spec=grid_spec,
        out_shape=jax.ShapeDtypeStruct((n_lat, d), BF16),
        compiler_params=_params(("arbitrary", "arbitrary"), 32 << 20),
        name="attention",
    )(sink, *([qkv] * 9))


def _rope_tables(seq_len, tm):
    rows = seq_len // GRID_W
    row = np.repeat(np.arange(rows, dtype=np.float32), GRID_W)
    col = np.tile(np.arange(GRID_W, dtype=np.float32), rows)
    axis_dim = HEAD_DIM // 2
    inv_freq = (np.float32(ROPE_BASE) ** (-np.arange(0, axis_dim, 2, dtype=np.float32) / np.float32(axis_dim)))
    ang_r = (row[:, None] * inv_freq[None, :]).astype(np.float32)
    ang_c = (col[:, None] * inv_freq[None, :]).astype(np.float32)
    cos = np.concatenate([np.cos(ang_r), np.cos(ang_c), np.cos(ang_r), np.cos(ang_c)], axis=1)
    sin = np.concatenate([-np.sin(ang_r), -np.sin(ang_c), np.sin(ang_r), np.sin(ang_c)], axis=1)
    cos = np.concatenate([cos, np.ones((tm, HEAD_DIM), np.float32)], axis=0)
    sin = np.concatenate([sin, np.zeros((tm, HEAD_DIM), np.float32)], axis=0)
    return jnp.asarray(cos, F32), jnp.asarray(sin, F32)


def _chan_dft_kernel(x_ref, g_ref, sh_ref, sc_ref, cs_ref, a_ref, b_ref, y_scr):
    _norm_mod_rows(x_ref, g_ref, sh_ref, sc_ref, y_scr, BF16)
    gw = cs_ref.shape[0]
    cs = cs_ref[...]
    for g in range(x_ref.shape[1] // gw):
        cols = slice(g * gw, (g + 1) * gw)
        r = jnp.dot(y_scr[:, cols], cs, preferred_element_type=F32)
        a_ref[:, cols] = r[:, :gw].astype(a_ref.dtype)
        b_ref[:, cols] = r[:, gw:].astype(b_ref.dtype)


def _dft_mats(n):
    k = np.arange(n, dtype=np.int64)
    ang = 2.0 * np.pi * ((k[:, None] * k[None, :]) % n).astype(np.float64) / n
    scale = 1.0 / math.sqrt(n)
    return np.cos(ang) * scale, np.sin(ang) * scale


def _chan_dft(x, gain, mod, tm):
    nt, d = x.shape
    gw = d // FNET_GROUPS
    c, s = _dft_mats(gw)
    cs = jnp.asarray(np.concatenate([c, s], axis=1), BF16)
    return pl.pallas_call(
        _chan_dft_kernel,
        grid=(nt // tm,),
        in_specs=[pl.BlockSpec((tm, d), lambda i: (i, 0)),
                  pl.BlockSpec((1, d), lambda i: (0, 0)),
                  mod.spec(0), mod.spec(1),
                  pl.BlockSpec((gw, 2 * gw), lambda i: (0, 0))],
        out_specs=[pl.BlockSpec((tm, d), lambda i: (i, 0)), pl.BlockSpec((tm, d), lambda i: (i, 0))],
        out_shape=[jax.ShapeDtypeStruct((nt, d), BF16), jax.ShapeDtypeStruct((nt, d), BF16)],
        scratch_shapes=[pltpu.VMEM((tm, d), BF16)],
        compiler_params=_params(("arbitrary",), 2 * tm * d * 4 + 5 * tm * d * 2 + (8 << 20)),
        name="fnet_channel_dft",
    )(x, gain.reshape(1, d), mod.table, mod.table, cs)


FFT_RADIX = 4


def _pos_fft_kernel(a_ref, b_ref, twc_ref, tws_ref, c_ref, s_ref, o_ref):
    q = a_ref.shape[0] // FFT_RADIX
    tn = a_ref.shape[1]
    ar = [a_ref[k * q:(k + 1) * q, :].astype(F32) for k in range(FFT_RADIX)]
    br = [b_ref[k * q:(k + 1) * q, :].astype(F32) for k in range(FFT_RADIX)]
    s02r, d02r, s13r, d13r = ar[0] + ar[2], ar[0] - ar[2], ar[1] + ar[3], ar[1] - ar[3]
    s02b, d02b, s13b, d13b = br[0] + br[2], br[0] - br[2], br[1] + br[3], br[1] - br[3]
    g = [(s02r + s13r, s02b + s13b), (d02r - d13b, d02b + d13r),
         (s02r - s13r, s02b - s13b), (d02r + d13b, d02b - d13r)]
    for j in range(FFT_RADIX):
        gr, gb = g[j]
        if j > 0:
            rows = slice(j * q, (j + 1) * q)
            tc = jnp.concatenate([twc_ref[rows, :]] * (tn // LANE), axis=1)
            ts = jnp.concatenate([tws_ref[rows, :]] * (tn // LANE), axis=1)
            gr, gb = gr * tc - gb * ts, gb * tc + gr * ts
        f = (jnp.dot(c_ref[...], gr.astype(BF16), preferred_element_type=F32)
             - jnp.dot(s_ref[...], gb.astype(BF16), preferred_element_type=F32))
        o_ref[j * q:(j + 1) * q, :] = f.astype(o_ref.dtype)


def _pos_fft(a, b, n_batch, seq_len):
    nt, d = a.shape
    q = seq_len // FFT_RADIX
    c, s = _dft_mats(q)
    scale = math.sqrt(q) / math.sqrt(seq_len)
    cq, sq = jnp.asarray(c * scale, BF16), jnp.asarray(s * scale, BF16)
    l2 = np.arange(q, dtype=np.int64)
    ang = np.concatenate([2.0 * np.pi * ((l2 * j) % seq_len) / seq_len for j in range(FFT_RADIX)])
    twc = jnp.asarray(np.broadcast_to(np.cos(ang)[:, None], (seq_len, LANE)), F32)
    tws = jnp.asarray(np.broadcast_to(np.sin(ang)[:, None], (seq_len, LANE)), F32)
    tn = _pow2_tile(256, d)
    return pl.pallas_call(
        _pos_fft_kernel,
        grid=(n_batch, d // tn),
        in_specs=[pl.BlockSpec((seq_len, tn), lambda bb, n: (bb, n)),
                  pl.BlockSpec((seq_len, tn), lambda bb, n: (bb, n)),
                  pl.BlockSpec((seq_len, LANE), lambda bb, n: (0, 0)),
                  pl.BlockSpec((seq_len, LANE), lambda bb, n: (0, 0)),
                  pl.BlockSpec((q, q), lambda bb, n: (0, 0)),
                  pl.BlockSpec((q, q), lambda bb, n: (0, 0))],
        out_specs=pl.BlockSpec((seq_len, tn), lambda bb, n: (bb, n)),
        out_shape=jax.ShapeDtypeStruct((nt, d), BF16),
        compiler_params=_params(("arbitrary", "arbitrary"),
                                6 * seq_len * tn * 2 + 4 * seq_len * LANE * 4 + 4 * q * q * 2
                                + 12 * seq_len * tn * 4 + (4 << 20)),
        name="fnet_position_fft",
    )(a, b, twc, tws, cq, sq)


def _resid_mm_interleave_kernel(f0_ref, f1_ref, f2_ref, f3_ref, p_ref, w_ref, x_ref, gt_ref, o_ref):
    fcat = jnp.concatenate([f[...] for f in (f0_ref, f1_ref, f2_ref, f3_ref)], axis=0)
    ftrue = jnp.dot(p_ref[...], fcat, preferred_element_type=F32).astype(BF16)
    o_ref[...] = x_ref[...] + gt_ref[0] * jnp.dot(ftrue, w_ref[...], preferred_element_type=F32)


def _resid_mm_interleave(f, w, x, mod, gate_chunk, seq_len, tm):
    n_rows, d = x.shape
    k = w.shape[0]
    q = seq_len // FFT_RADIX
    rj = tm // FFT_RADIX
    per_seq = seq_len // tm

    def f_map(j):
        return lambda i: ((i // per_seq) * (seq_len // rj) + j * (q // rj) + i % per_seq, 0)

    perm = np.zeros((tm, tm), np.float32)
    r = np.arange(rj)
    for j in range(FFT_RADIX):
        perm[FFT_RADIX * r + j, j * rj + r] = 1.0
    return pl.pallas_call(
        _resid_mm_interleave_kernel,
        grid=(n_rows // tm,),
        in_specs=[pl.BlockSpec((rj, k), f_map(j)) for j in range(FFT_RADIX)]
        + [pl.BlockSpec((tm, tm), lambda i: (0, 0)),
           pl.BlockSpec((k, d), lambda i: (0, 0)),
           pl.BlockSpec((tm, d), lambda i: (i, 0)),
           mod.spec(gate_chunk)],
        out_specs=pl.BlockSpec((tm, d), lambda i: (i, 0)),
        out_shape=jax.ShapeDtypeStruct((n_rows, d), F32),
        compiler_params=_params(("arbitrary",), 4 * tm * k * 2 + 2 * k * d * 2 + 6 * tm * d * 4 + (4 << 20)),
        name="fnet_out",
    )(f, f, f, f, jnp.asarray(perm, BF16), w, x, mod.table)


def kernel(x, c, ctx, c_ctx, ada_w, ada_b, norm_mix_g, norm_ffn_g, gm_w_in, gm_v_g, gm_w_s, gm_b_s, gm_w_out, cv_w_pw1, cv_b_pw1, cv_w_dw, cv_b_dw, cv_norm_g, cv_w_pw2, at_w_qkv, at_q_g, at_k_g, at_sink, at_w_o, ft_w_out, f_w_gate, f_w_up, f_w_down, m_w_router, m_b_router, m_w_gate, m_w_up, m_w_down):
    n_batch, seq_len, d = x.shape
    ctx_len = ctx.shape[1]
    depth = ada_w.shape[0]
    assert depth == 4 and seq_len % GRID_W == 0 and ctx_len % CHUNK == 0 and seq_len % CHUNK == 0
    assert d % (HEAD_DIM * KV_GROUP) == 0 and n_batch < 16
    n_lat, n_ctx = n_batch * seq_len, n_batch * ctx_len
    nt = n_lat + n_ctx
    tm = _pow2_tile(ROW_TILE, seq_len, n_ctx)
    tn = _pow2_tile(COL_TILE, d // KV_GROUP)
    tmd = _pow2_tile(DMA_ROWS, seq_len, n_ctx)
    tms = _pow2_tile(512, seq_len, n_ctx)
    bf = lambda w: w.astype(BF16)
    big = _cast_layer_bf16

    cc = jnp.zeros((16, d), F32).at[:n_batch].set(c).at[n_batch].set(c_ctx)
    table = _ada_table(cc, ada_w, ada_b).reshape(depth, 16 * 6, 1, d)
    mods = [_Mod(table[l], tm, seq_len, n_batch) for l in range(depth)]
    mods_s = [_Mod(table[l], tms, seq_len, n_batch) for l in range(depth)]
    mods_d = [_Mod(table[l], tmd, seq_len, n_batch) for l in range(depth)]

    x_lat, x_ctx = x.reshape(n_lat, d), ctx.reshape(n_ctx, d)

    h = _proj_call(functools.partial(_proj_gelu_kernel, n_lat // tm), (x_lat, x_ctx), norm_mix_g[0], mods[0], 0,
                   big(gm_w_in, 0), [], [], 2 * d, tm, tn, "gmlp_in")
    xs = _gmlp_tail(h, gm_v_g[0], gm_w_s[0], gm_b_s[0], bf(gm_w_out[0]), x_lat, x_ctx, mods_s[0], tms)
    xs = _ffn(xs, norm_ffn_g[0], mods[0], big(f_w_gate, 0), big(f_w_up, 0), big(f_w_down, 0), tm)

    w_pw1 = big(cv_w_pw1, 0)
    b_pw1 = cv_b_pw1[0].reshape(1, 2 * d)
    bias_specs = [pl.BlockSpec((1, tn), lambda i, j: (0, j)),
                  pl.BlockSpec((1, tn), lambda i, j: (0, j + d // tn))]
    z = _proj_call(_proj_glu_kernel, xs, norm_mix_g[1], mods[1], 0, w_pw1,
                   [b_pw1, b_pw1], bias_specs, d, tm, tn, "conv_in")
    tmc = _pow2_tile(256, seq_len, ctx_len)
    xs = _conv_tail(z, cv_w_dw[0], cv_b_dw[0], cv_norm_g[0], bf(cv_w_pw2[0]), xs,
                    _Mod(table[1], tmc, seq_len, n_batch), tmc, seq_len, n_lat, ctx_len)
    xs = _moe(xs, norm_ffn_g[1], mods[1], mods_d[1], m_w_router[0], m_b_router[0],
              big(m_w_gate, 0), big(m_w_up, 0), big(m_w_down, 0), tm, tmd)

    qkv_dim = at_w_qkv.shape[2]
    n_qk_tiles = (d + d // KV_GROUP) // tn
    cos, sin = _rope_tables(seq_len, tm)
    rope_block = lambda i, j: (jnp.where(i < n_lat // tm, i % (seq_len // tm), seq_len // tm), 0)
    q_gain = _rope_layout(at_q_g[0] * (HEAD_DIM ** -0.5 * LOG2E), 1)
    k_gain = _rope_layout(at_k_g[0], 1)
    gains = jnp.stack([q_gain] * (d // tn) + [k_gain] * (qkv_dim // tn - d // tn)).reshape(-1, 1, HEAD_DIM)
    qkv_specs = [pl.BlockSpec((1, 1, HEAD_DIM), lambda i, j: (j, 0, 0)),
                 pl.BlockSpec((tm, HEAD_DIM), rope_block), pl.BlockSpec((tm, HEAD_DIM), rope_block)]
    n_qk = d + d // KV_GROUP
    w_qkv = jnp.concatenate([_rope_layout(at_w_qkv[0][:, :n_qk], n_qk // HEAD_DIM), at_w_qkv[0][:, n_qk:]], axis=1)
    qkv = _proj_call(functools.partial(_proj_qkv_kernel, n_qk_tiles), xs, norm_mix_g[2], mods[2], 0,
                     bf(w_qkv), [gains, cos, sin], qkv_specs, qkv_dim, tm, tn, "qkv")
    o = _attention(qkv, at_sink[0], n_batch, seq_len, ctx_len, d)
    xl = _resid_mm(o, bf(at_w_o[0]), xs, mods_s[2], 2, n_lat, tms, "attn_out")
    xl = _ffn(xl, norm_ffn_g[2], mods[2], big(f_w_gate, 1), big(f_w_up, 1), big(f_w_down, 1), tm)

    a, b = _chan_dft(xl, norm_mix_g[3], mods[3], tm)
    f = _pos_fft(a, b, n_batch, seq_len)
    xl = _resid_mm_interleave(f, bf(ft_w_out[0]), xl, mods_s[3], 2, seq_len, tms)
    xl = _moe(xl, norm_ffn_g[3], mods[3], mods_d[3], m_w_router[1], m_b_router[1],
              big(m_w_gate, 1), big(m_w_up, 1), big(m_w_down, 1), tm, tmd)
    return xl.reshape(n_batch, seq_len, d)
```

```python
import functools
import math

import numpy as np
import jax
import jax.numpy as jnp
from jax import lax
from jax.experimental import pallas as pl
from jax.experimental.pallas import tpu as pltpu

F32 = jnp.float32
BF16 = jnp.bfloat16

EPS = 1e-6
GRID_W = 64
CHUNK = 128
GMLP_GROUPS = 8
CONV_WIDTH = 31
CONV_PAD = CONV_WIDTH // 2
HEAD_DIM = 128
KV_GROUP = 4
WINDOW = 128
ROPE_BASE = 10000.0
FNET_GROUPS = 8
TOP_K = 2
LOG2E = math.log2(math.e)

LANE = 128
V7X_VMEM_LIMIT = 56 << 20

ROW_TILE = 1024
COL_TILE = 512
CONV_HALO = 16
MOE_TILE = 1024
DMA_ROWS = 512
CAST_BLOCK_BYTES = 8 << 20


def _params(sem, vmem_bytes):
    return pltpu.CompilerParams(dimension_semantics=sem,
                                vmem_limit_bytes=int(min(max(vmem_bytes, 16 << 20), V7X_VMEM_LIMIT)))


def _pow2_tile(target, *sizes):
    t = target
    while any(s % t for s in sizes):
        t //= 2
    return t


def _sigmoid(x):
    return 1.0 / (1.0 + jnp.exp(-x))


def _norm_mod(x, g, shift, scale):
    ms = jnp.mean(x * x, axis=-1, keepdims=True)
    return (x * lax.rsqrt(ms + EPS) * g) * (1.0 + scale) + shift


def _norm_mod_rows(x_ref, g_ref, sh_ref, sc_ref, dst_ref, dtype, chunk=256):
    rows, d = x_ref.shape
    chunk = min(chunk, rows)
    schunk = min(4 * chunk, rows)
    gs = g_ref[...] * (1.0 + sc_ref[0])
    sh = sh_ref[0]

    def run(rs_scr):
        def stats(r, carry):
            sl = pl.ds(pl.multiple_of(r * schunk, schunk), schunk)
            xv = x_ref[sl, :]
            rs_scr[sl, :] = lax.rsqrt(jnp.sum(xv * xv, axis=-1, keepdims=True) * (1.0 / d) + EPS)
            return carry

        lax.fori_loop(0, rows // schunk, stats, 0)

        def apply(r, carry):
            sl = pl.ds(pl.multiple_of(r * chunk, chunk), chunk)
            dst_ref[sl, :] = (x_ref[sl, :] * rs_scr[sl, :] * gs + sh).astype(dtype)
            return carry

        lax.fori_loop(0, rows // chunk, apply, 0)

    pl.run_scoped(run, pltpu.VMEM((rows, 1), F32))


def _cast_kernel(x_ref, o_ref):
    o_ref[...] = x_ref[0].astype(o_ref.dtype)


def _cast_layer_bf16(w, layer):
    k, n = w.shape[-2:]
    rk = _pow2_tile(1 << int(math.log2(max(8, CAST_BLOCK_BYTES // (n * 4)))), k)
    if w.ndim == 4:
        grid = (w.shape[1], k // rk)
        in_spec = pl.BlockSpec((1, 1, rk, n), lambda e, r: (layer, e, r, 0))
        out_spec = pl.BlockSpec((1, rk, n), lambda e, r: (e, r, 0))
    else:
        grid = (k // rk,)
        in_spec = pl.BlockSpec((1, rk, n), lambda r: (layer, r, 0))
        out_spec = pl.BlockSpec((rk, n), lambda r: (r, 0))
    return pl.pallas_call(
        _cast_kernel,
        grid=grid,
        in_specs=[in_spec],
        out_specs=out_spec,
        out_shape=jax.ShapeDtypeStruct(w.shape[1:], BF16),
        compiler_params=_params(("arbitrary",) * len(grid), 4 * rk * n * 4 + (4 << 20)),
        name="cast_bf16",
    )(w)


def _ada_kernel(c_ref, w_ref, b_ref, o_ref):
    c = c_ref[...]
    s = (c * _sigmoid(c)).astype(BF16)
    o_ref[0] = jnp.dot(s, w_ref[0].astype(BF16), preferred_element_type=F32) + b_ref[0]


def _ada_table(cc, ada_w, ada_b):
    depth, d, n6 = ada_w.shape
    tn = _pow2_tile(1024, n6)
    rows = cc.shape[0]
    return pl.pallas_call(
        _ada_kernel,
        grid=(depth, n6 // tn),
        in_specs=[pl.BlockSpec((rows, d), lambda l, j: (0, 0)),
                  pl.BlockSpec((1, d, tn), lambda l, j: (l, 0, j)),
                  pl.BlockSpec((1, 1, tn), lambda l, j: (l, 0, j))],
        out_specs=pl.BlockSpec((1, rows, tn), lambda l, j: (l, 0, j)),
        out_shape=jax.ShapeDtypeStruct((depth, rows, n6), F32),
        compiler_params=_params(("arbitrary", "arbitrary"), 2 * d * tn * 4 + d * tn * 2 + (8 << 20)),
        name="ada_table",
    )(cc, ada_w, ada_b.reshape(depth, 1, n6))


class _Mod:
    def __init__(self, table, tm, seq_len, n_batch):
        self.table, self.tm, self.seq_len, self.n_batch = table, tm, seq_len, n_batch
        self.d = table.shape[-1]

    def spec(self, chunk):
        tm, seq_len, n_batch = self.tm, self.seq_len, self.n_batch

        def index(i, *_):
            return (jnp.minimum(i * tm // seq_len, n_batch) * 6 + chunk, 0, 0)

        return pl.BlockSpec((1, 1, self.d), index)


def _gelu_tanh(x):
    return 0.5 * x * (1.0 + jnp.tanh(math.sqrt(2.0 / math.pi) * (x + 0.044715 * (x * x * x))))


def _proj_gelu_kernel(n_lat_blocks, xl_ref, xc_ref, g_ref, sh_ref, sc_ref, w_ref, o_ref, y_scr):
    i = pl.program_id(0)

    @pl.when(i < n_lat_blocks)
    def _():
        _norm_mod_rows(xl_ref, g_ref, sh_ref, sc_ref, y_scr, BF16)

    @pl.when(i >= n_lat_blocks)
    def _():
        _norm_mod_rows(xc_ref, g_ref, sh_ref, sc_ref, y_scr, BF16)

    tn = w_ref.shape[2]
    for j in range(w_ref.shape[0]):
        acc = jnp.dot(y_scr[...], w_ref[j], preferred_element_type=F32)
        o_ref[:, j * tn:(j + 1) * tn] = _gelu_tanh(acc).astype(o_ref.dtype)


def _proj_glu_kernel(x_ref, g_ref, sh_ref, sc_ref, w_ref, b_ref, o_ref, y_scr):
    _norm_mod_rows(x_ref, g_ref, sh_ref, sc_ref, y_scr, BF16)
    tn = w_ref.shape[2]
    half = w_ref.shape[0] // 2
    for j in range(half):
        y = y_scr[...]
        a = jnp.dot(y, w_ref[j], preferred_element_type=F32) + b_ref[:, j * tn:(j + 1) * tn]
        gate = (jnp.dot(y, w_ref[j + half], preferred_element_type=F32)
                + b_ref[:, (j + half) * tn:(j + half + 1) * tn])
        o_ref[:, j * tn:(j + 1) * tn] = (a * _sigmoid(gate)).astype(o_ref.dtype)


def _rope_layout(t, n_heads):
    lead = t.shape[:-1]
    q = HEAD_DIM // 4
    t = t.reshape(*lead, n_heads, 2, 2, q)
    return jnp.swapaxes(t, -2, -3).reshape(*lead, n_heads * HEAD_DIM)


def _rope_partner(x):
    return pltpu.roll(x, HEAD_DIM // 2, 1)


def _qkv_kernel(n_qk_tiles, x_ref, g_ref, sh_ref, sc_ref, w_ref, hg_ref, cos_ref, sin_ref, o_ref, y_scr):
    j = pl.program_id(1)

    @pl.when(j == 0)
    def _():
        _norm_mod_rows(x_ref, g_ref, sh_ref, sc_ref, y_scr, BF16)

    acc = jnp.dot(y_scr[...], w_ref[j], preferred_element_type=F32)

    @pl.when(j < n_qk_tiles)
    def _():
        hg, cos, sin = hg_ref[0], cos_ref[...], sin_ref[...]
        for h in range(acc.shape[1] // HEAD_DIM):
            t = acc[:, h * HEAD_DIM:(h + 1) * HEAD_DIM]
            t = t * lax.rsqrt(jnp.mean(t * t, axis=-1, keepdims=True) + EPS) * hg
            t = t * cos + _rope_partner(t) * sin
            o_ref[:, h * HEAD_DIM:(h + 1) * HEAD_DIM] = t.astype(o_ref.dtype)

    @pl.when(j >= n_qk_tiles)
    def _():
        o_ref[...] = acc.astype(o_ref.dtype)


def _qkv_proj(x, gain, mod, w, gains, cos, sin, rope_block, n_qk_tiles, tm, tn):
    nt, d = x.shape
    n_tiles = w.shape[1] // tn
    w_tiles = jnp.swapaxes(w.reshape(d, n_tiles, tn), 0, 1)
    return pl.pallas_call(
        functools.partial(_qkv_kernel, n_qk_tiles),
        grid=(nt // tm, n_tiles),
        in_specs=[pl.BlockSpec((tm, d), lambda i, j: (i, 0)),
                  pl.BlockSpec((1, d), lambda i, j: (0, 0)),
                  mod.spec(0), mod.spec(1),
                  pl.BlockSpec((n_tiles, d, tn), lambda i, j: (0, 0, 0), pipeline_mode=pl.Buffered(1)),
                  pl.BlockSpec((1, 1, HEAD_DIM), lambda i, j: (j, 0, 0)),
                  pl.BlockSpec((tm, HEAD_DIM), lambda i, j: rope_block(i)),
                  pl.BlockSpec((tm, HEAD_DIM), lambda i, j: rope_block(i))],
        out_specs=pl.BlockSpec((None, tm, tn), lambda i, j: (j, i, 0)),
        out_shape=jax.ShapeDtypeStruct((n_tiles, nt, tn), BF16),
        scratch_shapes=[pltpu.VMEM((tm, d), BF16)],
        compiler_params=_params(("arbitrary", "arbitrary"),
                                2 * tm * d * 4 + tm * d * 2 + d * n_tiles * tn * 2 + 10 * tm * tn * 4 + (6 << 20)),
        name="qkv",
    )(x, gain.reshape(1, d), mod.table, mod.table, w_tiles, gains, cos, sin)


def _proj_call(kernel, x, gain, mod, chunk0, w, extra_in, extra_specs, n_out, tm, tn, name):
    if isinstance(x, tuple):
        x_lat, x_ctx = x
        nlb = x_lat.shape[0] // tm
        nt, d = x_lat.shape[0] + x_ctx.shape[0], x_lat.shape[1]
        in_specs = [pl.BlockSpec((tm, d), lambda i: (jnp.minimum(i, nlb - 1), 0)),
                    pl.BlockSpec((tm, d), lambda i: (jnp.maximum(i - nlb, 0), 0))]
        args = [x_lat, x_ctx]
    else:
        nt, d = x.shape
        in_specs = [pl.BlockSpec((tm, d), lambda i: (i, 0))]
        args = [x]
    in_specs += [pl.BlockSpec((1, d), lambda i: (0, 0)), mod.spec(chunk0), mod.spec(chunk0 + 1)]
    args += [gain.reshape(1, d), mod.table, mod.table]
    n_x = len(args)
    n_w = w.shape[1]
    w_tiles = jnp.swapaxes(w.reshape(d, n_w // tn, tn), 0, 1)
    in_specs.append(pl.BlockSpec((n_w // tn, d, tn), lambda i: (0, 0, 0), pipeline_mode=pl.Buffered(1)))
    args.append(w_tiles)
    in_specs += extra_specs
    args += extra_in
    vmem = 2 * n_x * tm * d * 4 + tm * d * 2 + d * n_w * 2 + 2 * tm * n_out * 2 + 8 * tm * tn * 4
    return pl.pallas_call(
        kernel,
        grid=(nt // tm,),
        in_specs=in_specs,
        out_specs=pl.BlockSpec((tm, n_out), lambda i: (i, 0)),
        out_shape=jax.ShapeDtypeStruct((nt, n_out), BF16),
        scratch_shapes=[pltpu.VMEM((tm, d), BF16)],
        compiler_params=_params(("arbitrary",), vmem + (4 << 20)),
        name=name,
    )(*args)


def _resid_mm_kernel(a_ref, w_ref, x_ref, gt_ref, o_ref):
    o_ref[...] = x_ref[...] + gt_ref[0] * jnp.dot(a_ref[...], w_ref[...], preferred_element_type=F32)


def _resid_mm(a, w, x, mod, gate_chunk, n_rows, tm, name):
    k, d = w.shape
    return pl.pallas_call(
        _resid_mm_kernel,
        grid=(n_rows // tm,),
        in_specs=[pl.BlockSpec((tm, k), lambda i: (i, 0)),
                  pl.BlockSpec((k, d), lambda i: (0, 0)),
                  pl.BlockSpec((tm, d), lambda i: (i, 0)),
                  mod.spec(gate_chunk)],
        out_specs=pl.BlockSpec((tm, d), lambda i: (i, 0)),
        out_shape=jax.ShapeDtypeStruct((n_rows, d), F32),
        compiler_params=_params(("arbitrary",), 2 * tm * k * 2 + 2 * k * d * 2 + 5 * tm * d * 4 + (4 << 20)),
        name=name,
    )(a, w, x, mod.table)


def _gmlp_tail_kernel(n_lat_blocks, h_ref, vg_ref, ws_ref, bs_ref, w_ref, xl_ref, xc_ref, gt_ref, o_ref, z_scr):
    tm, d = xl_ref.shape
    gw = d // GMLP_GROUPS
    v = h_ref[:, d:].astype(F32)
    vn = (v * lax.rsqrt(jnp.mean(v * v, axis=-1, keepdims=True) + EPS) * vg_ref[...]).astype(BF16)
    for c in range(tm // CHUNK):
        rows = slice(c * CHUNK, (c + 1) * CHUNK)
        for g in range(GMLP_GROUPS):
            cols = slice(g * gw, (g + 1) * gw)
            sv = jnp.dot(ws_ref[g], vn[rows, cols], preferred_element_type=F32) + bs_ref[g]
            z_scr[rows, cols] = (h_ref[rows, cols].astype(F32) * sv).astype(BF16)
    x = jnp.where(pl.program_id(0) < n_lat_blocks, xl_ref[...], xc_ref[...])
    o_ref[...] = x + gt_ref[0] * jnp.dot(z_scr[...], w_ref[...], preferred_element_type=F32)


def _gmlp_tail(h, v_g, w_s, b_s, w_out, x_lat, x_ctx, mod, tm):
    d = x_lat.shape[1]
    nt = x_lat.shape[0] + x_ctx.shape[0]
    nlb = x_lat.shape[0] // tm
    bs_tile = jnp.broadcast_to(b_s[:, :, None], (GMLP_GROUPS, CHUNK, d // GMLP_GROUPS)).astype(F32)
    return pl.pallas_call(
        functools.partial(_gmlp_tail_kernel, nlb),
        grid=(nt // tm,),
        in_specs=[pl.BlockSpec((tm, 2 * d), lambda i: (i, 0)),
                  pl.BlockSpec((1, d), lambda i: (0, 0)),
                  pl.BlockSpec((GMLP_GROUPS, CHUNK, CHUNK), lambda i: (0, 0, 0)),
                  pl.BlockSpec((GMLP_GROUPS, CHUNK, d // GMLP_GROUPS), lambda i: (0, 0, 0)),
                  pl.BlockSpec((d, d), lambda i: (0, 0)),
                  pl.BlockSpec((tm, d), lambda i: (jnp.minimum(i, nlb - 1), 0)),
                  pl.BlockSpec((tm, d), lambda i: (jnp.maximum(i - nlb, 0), 0)),
                  mod.spec(2)],
        out_specs=pl.BlockSpec((tm, d), lambda i: (i, 0)),
        out_shape=jax.ShapeDtypeStruct((nt, d), F32),
        scratch_shapes=[pltpu.VMEM((tm, d), BF16)],
        compiler_params=_params(("arbitrary",), 2 * tm * 2 * d * 2 + 2 * d * d * 2 + 8 * tm * d * 4 + (6 << 20)),
        name="gmlp_tail",
    )(h, v_g.reshape(1, d), w_s.astype(BF16), bs_tile, w_out, x_lat, x_ctx, mod.table)


def _conv_tail_kernel(edges, z_ref, zp_ref, zn_ref, wdw_ref, bdw_ref, ng_ref, w_ref, x_ref, gt_ref,
                      o_ref, ext_scr, cv_scr, sh_scr):
    i = pl.program_id(0)
    tm, d = x_ref.shape
    first, last = edges(i)
    ext_scr[0:CONV_HALO, :] = jnp.where(first, 0.0, zp_ref[...].astype(F32))
    ext_scr[CONV_HALO:CONV_HALO + tm, :] = z_ref[...].astype(F32)
    ext_scr[CONV_HALO + tm:, :] = jnp.where(last, 0.0, zn_ref[...].astype(F32))

    rc = min(64, tm)
    cc = sh_scr.shape[2]
    sh_rows = sh_scr.shape[1]

    def col_body(c, carry):
        cols = pl.ds(pl.multiple_of(c * cc, cc), cc)
        wts = wdw_ref[:, cols]
        bias = bdw_ref[:, cols]
        for b in range(1, 8):
            sh_scr[b] = ext_scr[b:b + sh_rows, cols]
        for r in range(tm // rc):
            acc = jnp.zeros((rc, cc), F32) + bias
            for k in range(CONV_WIDTH):
                off = r * rc + CONV_HALO - CONV_PAD + k
                b, a = off % 8, off - off % 8
                src = ext_scr[a:a + rc, cols] if b == 0 else sh_scr[b, a:a + rc, :]
                acc = acc + wts[k:k + 1, :] * src
            cv_scr[r * rc:(r + 1) * rc, cols] = acc
        return carry

    lax.fori_loop(0, d // cc, col_body, 0)
    cv = cv_scr[...]
    t = cv * lax.rsqrt(jnp.mean(cv * cv, axis=-1, keepdims=True) + EPS) * ng_ref[...]
    t = (t * _sigmoid(t)).astype(BF16)
    o_ref[...] = x_ref[...] + gt_ref[0] * jnp.dot(t, w_ref[...], preferred_element_type=F32)


def _conv_tail(z, w_dw, b_dw, n_g, w_pw2, x, mod, tm, seq_len, n_lat, ctx_len):
    nt, d = x.shape
    hb = tm // CONV_HALO
    n_halo_blocks = nt // CONV_HALO
    lat_blocks, seq_blocks, ctx_blocks = n_lat // tm, seq_len // tm, ctx_len // tm

    def edges(i):
        in_lat = i < lat_blocks
        pos = jnp.where(in_lat, i % seq_blocks, (i - lat_blocks) % ctx_blocks)
        per = jnp.where(in_lat, seq_blocks, ctx_blocks)
        return pos == 0, pos == per - 1

    kdw = w_dw.shape[0]
    kpad = -(-kdw // 8) * 8
    w_dw_p = jnp.zeros((kpad, d), F32).at[:kdw].set(w_dw)
    return pl.pallas_call(
        functools.partial(_conv_tail_kernel, edges),
        grid=(nt // tm,),
        in_specs=[pl.BlockSpec((tm, d), lambda i: (i, 0)),
                  pl.BlockSpec((CONV_HALO, d), lambda i: (jnp.maximum(i * hb - 1, 0), 0)),
                  pl.BlockSpec((CONV_HALO, d), lambda i: (jnp.minimum((i + 1) * hb, n_halo_blocks - 1), 0)),
                  pl.BlockSpec((kpad, d), lambda i: (0, 0)),
                  pl.BlockSpec((1, d), lambda i: (0, 0)),
                  pl.BlockSpec((1, d), lambda i: (0, 0)),
                  pl.BlockSpec((d, d), lambda i: (0, 0)),
                  pl.BlockSpec((tm, d), lambda i: (i, 0)),
                  mod.spec(2)],
        out_specs=pl.BlockSpec((tm, d), lambda i: (i, 0)),
        out_shape=jax.ShapeDtypeStruct((nt, d), F32),
        scratch_shapes=[pltpu.VMEM((tm + 2 * CONV_HALO, d), F32), pltpu.VMEM((tm, d), F32),
                        pltpu.VMEM((8, tm + 2 * CONV_HALO - 8, min(256, d)), F32)],
        compiler_params=_params(("arbitrary",), 2 * d * d * 2 + 12 * tm * d * 4 + (6 << 20)),
        name="conv_tail",
    )(z, z, z, w_dw_p, b_dw.reshape(1, d), n_g.reshape(1, d), w_pw2, x, mod.table)


def _copy_rows(src_ref, dst_ref, chunk=256):
    rows = src_ref.shape[0]
    chunk = min(chunk, rows)

    def body(r, carry):
        sl = pl.ds(pl.multiple_of(r * chunk, chunk), chunk)
        dst_ref[sl, :] = src_ref[sl, :].astype(dst_ref.dtype)
        return carry

    lax.fori_loop(0, rows // chunk, body, 0)


def _swiglu_partial(y, wg, wu, wd):
    hg = jnp.dot(y, wg, preferred_element_type=F32)
    hu = jnp.dot(y, wu, preferred_element_type=F32)
    h = (hg * _sigmoid(hg) * hu).astype(BF16)
    return jnp.dot(h, wd, preferred_element_type=F32)


def _ffn_kernel(x_ref, g_ref, sh_ref, sc_ref, gt_ref, wg_ref, wu_ref, wd_ref, o_ref, y_scr):
    f = pl.program_id(1)

    @pl.when(f == 0)
    def _():
        _norm_mod_rows(x_ref, g_ref, sh_ref, sc_ref, y_scr, BF16)
        o_ref[...] = x_ref[...] + gt_ref[0] * _swiglu_partial(y_scr[...], wg_ref[...], wu_ref[...], wd_ref[...])

    @pl.when(f > 0)
    def _():
        o_ref[...] += gt_ref[0] * _swiglu_partial(y_scr[...], wg_ref[...], wu_ref[...], wd_ref[...])


def _ffn(x, gain, mod, w_gate, w_up, w_down, tm):
    nt, d = x.shape
    dff = w_gate.shape[1]
    tf = 512 if dff % 512 == 0 else 256
    while dff % tf:
        tf //= 2
    vmem = 4 * tm * d * 4 + tm * d * 2 + 6 * d * tf * 2 + 5 * tm * tf * 4
    return pl.pallas_call(
        _ffn_kernel,
        grid=(nt // tm, dff // tf),
        in_specs=[pl.BlockSpec((tm, d), lambda i, f: (i, 0)),
                  pl.BlockSpec((1, d), lambda i, f: (0, 0)),
                  mod.spec(3), mod.spec(4), mod.spec(5),
                  pl.BlockSpec((d, tf), lambda i, f: (0, f)),
                  pl.BlockSpec((d, tf), lambda i, f: (0, f)),
                  pl.BlockSpec((tf, d), lambda i, f: (f, 0))],
        out_specs=pl.BlockSpec((tm, d), lambda i, f: (i, 0)),
        out_shape=jax.ShapeDtypeStruct((nt, d), F32),
        scratch_shapes=[pltpu.VMEM((tm, d), BF16)],
        compiler_params=_params(("arbitrary", "arbitrary"), vmem + (4 << 20)),
        name="ffn_dense",
    )(x, gain.reshape(1, d), mod.table, mod.table, mod.table, w_gate, w_up, w_down)


def _route_kernel(n_experts, x_ref, g_ref, sh_ref, sc_ref, wh_ref, wl_ref, br_ref, tri_ref,
                  y_ref, info_ref, cnt_ref, carry):
    i = pl.program_id(0)

    @pl.when(i == 0)
    def _():
        carry[...] = jnp.zeros_like(carry)

    _norm_mod_rows(x_ref, g_ref, sh_ref, sc_ref, y_ref, F32)
    y = y_ref[...]
    yh = y.astype(BF16)
    yl = (y - yh.astype(F32)).astype(BF16)
    wh, wl = wh_ref[...], wl_ref[...]
    lg = (jnp.dot(yh, wh, preferred_element_type=F32) + jnp.dot(yh, wl, preferred_element_type=F32)
          + jnp.dot(yl, wh, preferred_element_type=F32) + br_ref[...])
    lane = lax.broadcasted_iota(jnp.int32, lg.shape, 1)
    neg = jnp.float32(-jnp.inf)
    lg = jnp.where(lane < n_experts, lg, neg)
    m1 = jnp.max(lg, axis=-1, keepdims=True)
    i1 = jnp.min(jnp.where(lg == m1, lane, LANE), axis=-1, keepdims=True)
    lg2 = jnp.where(lane == i1, neg, lg)
    m2 = jnp.max(lg2, axis=-1, keepdims=True)
    i2 = jnp.min(jnp.where(lg2 == m2, lane, LANE), axis=-1, keepdims=True)
    e = jnp.exp(m2 - m1)
    w1 = 1.0 / (1.0 + e)
    w2 = e * w1
    hit1, hit2 = lane == i1, lane == i2
    onehot = (hit1 | hit2).astype(F32)
    pre = jnp.dot(tri_ref[...], onehot.astype(BF16), preferred_element_type=F32) + carry[0:1, :]
    r1 = jnp.sum(jnp.where(hit1, pre, 0.0), axis=-1, keepdims=True)
    r2 = jnp.sum(jnp.where(hit2, pre, 0.0), axis=-1, keepdims=True)
    total = carry[0:1, :] + jnp.sum(onehot, axis=0, keepdims=True)
    carry[...] = jnp.broadcast_to(total, carry.shape)
    cnt_ref[...] = jnp.broadcast_to(total, cnt_ref.shape)
    info = jnp.where(lane == 0, i1.astype(F32), 0.0)
    info = jnp.where(lane == 1, i2.astype(F32), info)
    info = jnp.where(lane == 2, w1, info)
    info = jnp.where(lane == 3, w2, info)
    info = jnp.where(lane == 4, r1, info)
    info = jnp.where(lane == 5, r2, info)
    info_ref[...] = info


def _route(x, gain, mod, w_router, b_router, tm):
    nt, d = x.shape
    n_experts = w_router.shape[1]
    wr = jnp.zeros((d, LANE), F32).at[:, :n_experts].set(w_router)
    wr_hi = wr.astype(BF16)
    wr_lo = (wr - wr_hi.astype(F32)).astype(BF16)
    br = jnp.zeros((1, LANE), F32).at[0, :n_experts].set(b_router)
    tri = jnp.tril(jnp.ones((tm, tm), BF16), -1)
    return pl.pallas_call(
        functools.partial(_route_kernel, n_experts),
        grid=(nt // tm,),
        in_specs=[pl.BlockSpec((tm, d), lambda i: (i, 0)),
                  pl.BlockSpec((1, d), lambda i: (0, 0)),
                  mod.spec(3), mod.spec(4),
                  pl.BlockSpec((d, LANE), lambda i: (0, 0)),
                  pl.BlockSpec((d, LANE), lambda i: (0, 0)),
                  pl.BlockSpec((1, LANE), lambda i: (0, 0)),
                  pl.BlockSpec((tm, tm), lambda i: (0, 0))],
        out_specs=[pl.BlockSpec((tm, d), lambda i: (i, 0)),
                   pl.BlockSpec((tm, LANE), lambda i: (i, 0)),
                   pl.BlockSpec((8, LANE), lambda i: (0, 0))],
        out_shape=[jax.ShapeDtypeStruct((nt, d), F32),
                   jax.ShapeDtypeStruct((nt, LANE), F32),
                   jax.ShapeDtypeStruct((8, LANE), F32)],
        scratch_shapes=[pltpu.VMEM((8, LANE), F32)],
        compiler_params=_params(("arbitrary",), 4 * tm * d * 4 + 2 * tm * d * 2 + 2 * tm * tm * 2 + (8 << 20)),
        name="moe_route",
    )(x, gain.reshape(1, d), mod.table, mod.table, wr_hi, wr_lo, br, tri)


def _expert_kernel(n_f, blk_ref, nused_ref, rowtok_ref, y_hbm, wg_ref, wu_ref, wd_ref, o_ref, xbuf, y_scr, sems):
    del blk_ref
    i, f = pl.program_id(0), pl.program_id(1)
    tm = o_ref.shape[0]
    chunk = tm // n_f
    n_used = nused_ref[0]
    used = i < n_used
    slot = i % 2
    nxt = jnp.minimum(i + 1, n_used - 1)

    def row_copy(blk, slot_, row):
        tok = rowtok_ref[blk * tm + row]
        return pltpu.make_async_copy(y_hbm.at[pl.ds(tok, 1), :], xbuf.at[slot_, pl.ds(row, 1), :], sems.at[slot_])

    def copy_rows(blk, slot_, first_row, n, start):
        def body(r, carry):
            cp = row_copy(blk, slot_, first_row + r)
            cp.start() if start else cp.wait()
            return carry

        lax.fori_loop(0, n, body, 0)

    @pl.when(jnp.logical_and(i == 0, f == 0))
    def _():
        copy_rows(0, 0, 0, tm, True)
        copy_rows(0, 0, 0, tm, False)

    def start_chunk():
        for r in range(chunk):
            row_copy(nxt, 1 - slot, f * chunk + r).start()

    lag = min(2, n_f)

    @pl.when(jnp.logical_and(used, f == 0))
    def _():
        @pl.when(i > 0)
        def _():
            copy_rows(i, slot, (n_f - lag) * chunk, lag * chunk, False)

        _copy_rows(xbuf.at[slot], y_scr)
        o_ref[...] = _swiglu_partial(y_scr[...], wg_ref[0], wu_ref[0], wd_ref[0])
        start_chunk()

    @pl.when(jnp.logical_and(used, f > 0))
    def _():
        @pl.when(f >= lag)
        def _():
            for r in range(chunk):
                row_copy(nxt, 1 - slot, (f - lag) * chunk + r).wait()

        o_ref[...] += _swiglu_partial(y_scr[...], wg_ref[0], wu_ref[0], wd_ref[0])
        start_chunk()

    @pl.when(jnp.logical_and(i == n_used - 1, f == n_f - 1))
    def _():
        copy_rows(nxt, 1 - slot, (n_f - lag) * chunk, lag * chunk, False)

    @pl.when(jnp.logical_and(jnp.logical_not(used), f == 0))
    def _():
        o_ref[...] = jnp.zeros_like(o_ref)


def _experts(y, row_tok, block_e, n_used, w_gate, w_up, w_down, tm):
    n_rows = row_tok.shape[0]
    d = y.shape[1]
    dff = w_gate.shape[2]
    tf = _pow2_tile(512, dff)
    n_f = dff // tf
    assert tm % n_f == 0
    vmem = 4 * tm * d * 4 + tm * d * 2 + 6 * d * tf * 2 + 5 * tm * tf * 4
    grid_spec = pltpu.PrefetchScalarGridSpec(
        num_scalar_prefetch=3,
        grid=(n_rows // tm, n_f),
        in_specs=[pl.BlockSpec(memory_space=pl.ANY),
                  pl.BlockSpec((1, d, tf), lambda i, f, be, nu, rt: (be[i], 0, f)),
                  pl.BlockSpec((1, d, tf), lambda i, f, be, nu, rt: (be[i], 0, f)),
                  pl.BlockSpec((1, tf, d), lambda i, f, be, nu, rt: (be[i], f, 0))],
        out_specs=pl.BlockSpec((tm, d), lambda i, f, be, nu, rt: (i, 0)),
        scratch_shapes=[pltpu.VMEM((2, tm, d), F32), pltpu.VMEM((tm, d), BF16), pltpu.SemaphoreType.DMA((2,))],
    )
    return pl.pallas_call(
        functools.partial(_expert_kernel, n_f),
        grid_spec=grid_spec,
        out_shape=jax.ShapeDtypeStruct((n_rows, d), F32),
        compiler_params=_params(("arbitrary", "arbitrary"), vmem + (4 << 20)),
        name="moe_experts",
    )(block_e, n_used, row_tok, y, w_gate, w_up, w_down)


def _combine_kernel(dest_hbm, ys_hbm, x_ref, gt_ref, info_ref, o_ref, idx_smem, buf, sem_idx, sem_rows):
    i = pl.program_id(0)
    rows = x_ref.shape[0]

    def row_copy(slot, t, s):
        d = idx_smem[slot, TOP_K * t + s]
        return pltpu.make_async_copy(ys_hbm.at[pl.ds(d, 1), :], buf.at[slot, s, pl.ds(t, 1), :], sem_rows.at[slot])

    def fetch(step, slot):
        idx_copy = pltpu.make_async_copy(dest_hbm.at[step], idx_smem.at[slot], sem_idx)
        idx_copy.start()
        idx_copy.wait()

        def issue(t, carry):
            for s in range(TOP_K):
                row_copy(slot, t, s).start()
            return carry

        lax.fori_loop(0, rows, issue, 0)

    def step(slot):
        @pl.when(i + 1 < pl.num_programs(0))
        def _():
            fetch(i + 1, 1 - slot)

        def drain(t, carry):
            for s in range(TOP_K):
                row_copy(slot, t, s).wait()
            return carry

        lax.fori_loop(0, rows, drain, 0, unroll=8)
        info = info_ref[...]
        mix = info[:, 2:3] * buf[slot, 0] + info[:, 3:4] * buf[slot, 1]
        o_ref[...] = x_ref[...] + gt_ref[0] * mix

    @pl.when(i == 0)
    def _():
        fetch(0, 0)

    for slot in range(2):
        pl.when(i % 2 == slot)(functools.partial(step, slot))


def _combine(ys, dest, x, mod, info, tmd):
    nt, d = x.shape
    return pl.pallas_call(
        _combine_kernel,
        grid=(nt // tmd,),
        in_specs=[pl.BlockSpec(memory_space=pl.ANY),
                  pl.BlockSpec(memory_space=pl.ANY),
                  pl.BlockSpec((tmd, d), lambda i: (i, 0)),
                  mod.spec(5),
                  pl.BlockSpec((tmd, LANE), lambda i: (i, 0))],
        out_specs=pl.BlockSpec((tmd, d), lambda i: (i, 0)),
        out_shape=jax.ShapeDtypeStruct((nt, d), F32),
        scratch_shapes=[pltpu.SMEM((2, TOP_K * tmd), jnp.int32),
                        pltpu.VMEM((2, TOP_K, tmd, d), F32),
                        pltpu.SemaphoreType.DMA(()), pltpu.SemaphoreType.DMA((2,))],
        compiler_params=_params(("arbitrary",), 12 * tmd * d * 4 + (8 << 20)),
        name="moe_combine",
    )(dest.reshape(nt // tmd, TOP_K * tmd), ys, x, mod.table, info)


def _moe(x, gain, mod_route, mod_comb, w_router, b_router, w_gate, w_up, w_down, tm, tmd):
    nt, d = x.shape
    n_experts = w_router.shape[1]
    y, info, cnt = _route(x, gain, mod_route, w_router, b_router, tm)
    counts = cnt[0, :n_experts].astype(jnp.int32)
    padded = (counts + MOE_TILE - 1) // MOE_TILE * MOE_TILE
    pend = jnp.cumsum(padded)
    pstart = pend - padded
    n_rows = (-(-(nt * TOP_K) // MOE_TILE) + n_experts) * MOE_TILE
    nb = n_rows // MOE_TILE
    e_idx = info[:, 0:TOP_K].astype(jnp.int32)
    rank = info[:, 4:4 + TOP_K].astype(jnp.int32)
    dest = (pstart[e_idx] + rank).reshape(-1)
    block_e = jnp.minimum(jnp.searchsorted(pend, jnp.arange(nb, dtype=jnp.int32) * MOE_TILE, side='right'),
                          n_experts - 1).astype(jnp.int32)
    n_used = (pend[-1:] // MOE_TILE).astype(jnp.int32)
    tok = jnp.repeat(jnp.arange(nt, dtype=jnp.int32), TOP_K)
    row_tok = jnp.zeros((n_rows,), jnp.int32).at[dest].set(tok, unique_indices=True)
    ys = _experts(y, row_tok, block_e, n_used, w_gate, w_up, w_down, MOE_TILE)
    return _combine(ys, dest, x, mod_comb, info, tmd)


def _attn_kernel(seq_len, n_q_tiles, sink_ref, *refs):
    q_refs = refs[:n_q_tiles]
    kp_ref, km_ref, kn_ref, kc_ref, vp_ref, vm_ref, vn_ref, vc_ref, o_ref = refs[n_q_tiles:]
    i = pl.program_id(1)
    tq = o_ref.shape[0]
    tile_w = km_ref.shape[1]
    n_kv = tile_w // HEAD_DIM
    n_ctx = kc_ref.shape[0]
    span = tq + 2 * WINDOW
    q_pos = i * tq + lax.broadcasted_iota(jnp.int32, (tq, span), 0)
    k_pos = i * tq - WINDOW + lax.broadcasted_iota(jnp.int32, (tq, span), 1)
    ok = (jnp.abs(q_pos - k_pos) <= WINDOW) & (k_pos >= 0) & (k_pos < seq_len)
    bias = jnp.where(ok, 0.0, -1e30).astype(F32)
    ones_b = jnp.ones((span, HEAD_DIM), BF16)
    ones_c = jnp.ones((n_ctx, HEAD_DIM), BF16)
    nt_dims = (((1,), (1,)), ((), ()))
    for kh in range(n_kv):
        hs = slice(kh * HEAD_DIM, (kh + 1) * HEAD_DIM)
        k_band = jnp.concatenate([kp_ref[:, hs], km_ref[:, hs], kn_ref[:, hs]], axis=0)
        k_ctx = kc_ref[:, hs]
        v_band = jnp.concatenate([jnp.concatenate([vp_ref[:, hs], vm_ref[:, hs], vn_ref[:, hs]], axis=0), ones_b],
                                 axis=1)
        v_ctx = jnp.concatenate([vc_ref[:, hs], ones_c], axis=1)
        for g in range(KV_GROUP):
            h = kh * KV_GROUP + g
            qs = slice(h * HEAD_DIM, (h + 1) * HEAD_DIM)
            lane0 = h * HEAD_DIM
            q = q_refs[lane0 // tile_w][:, lane0 % tile_w:lane0 % tile_w + HEAD_DIM]
            s_b = lax.dot_general(q, k_band, nt_dims, preferred_element_type=F32) + bias
            s_c = lax.dot_general(q, k_ctx, nt_dims, preferred_element_type=F32)
            sink = sink_ref[h] * LOG2E
            m = jnp.maximum(jnp.maximum(jnp.max(s_b, axis=-1, keepdims=True),
                                        jnp.max(s_c, axis=-1, keepdims=True)), sink)
            p_b = jnp.exp2(s_b - m).astype(BF16)
            p_c = jnp.exp2(s_c - m).astype(BF16)
            acc = (jnp.dot(p_b, v_band, preferred_element_type=F32)
                   + jnp.dot(p_c, v_ctx, preferred_element_type=F32))
            denom = acc[:, HEAD_DIM:HEAD_DIM + 1] + jnp.exp2(sink - m)
            o_ref[:, qs] = (acc[:, :HEAD_DIM] / denom).astype(o_ref.dtype)


def _attention(qkv, sink, n_batch, seq_len, ctx_len, d):
    tq = _pow2_tile(256, seq_len)
    kvw = qkv.shape[2]
    n_q_tiles = d // kvw
    assert kvw == d // KV_GROUP and qkv.shape[0] == n_q_tiles + 2
    wb = WINDOW
    qb, sb = seq_len // tq, seq_len // wb
    n_lat = n_batch * seq_len
    ctx_block0 = n_lat // ctx_len

    def prev_map(tile):
        return lambda b, i, s: (tile, b * sb + jnp.maximum(i * (tq // wb) - 1, 0), 0)

    def main_map(tile):
        return lambda b, i, s: (tile, b * qb + i, 0)

    def next_map(tile):
        return lambda b, i, s: (tile, b * sb + jnp.minimum((i + 1) * (tq // wb), sb - 1), 0)

    def ctx_map(tile):
        return lambda b, i, s: (tile, ctx_block0 + b, 0)

    q_specs = [pl.BlockSpec((None, tq, kvw), main_map(t)) for t in range(n_q_tiles)]
    kv_specs = []
    for tile in (n_q_tiles, n_q_tiles + 1):
        kv_specs += [pl.BlockSpec((None, wb, kvw), prev_map(tile)), pl.BlockSpec((None, tq, kvw), main_map(tile)),
                     pl.BlockSpec((None, wb, kvw), next_map(tile)), pl.BlockSpec((None, ctx_len, kvw), ctx_map(tile))]
    grid_spec = pltpu.PrefetchScalarGridSpec(
        num_scalar_prefetch=1,
        grid=(n_batch, qb),
        in_specs=q_specs + kv_specs,
        out_specs=pl.BlockSpec((tq, d), lambda b, i, s: (b * qb + i, 0)),
    )
    return pl.pallas_call(
        functools.partial(_attn_kernel, seq_len, n_q_tiles),
        grid_spec=grid_spec,
        out_shape=jax.ShapeDtypeStruct((n_lat, d), BF16),
        compiler_params=_params(("arbitrary", "arbitrary"), 32 << 20),
        name="attention",
    )(sink, *([qkv] * (n_q_tiles + 8)))


def _rope_tables(seq_len, tm):
    rows = seq_len // GRID_W
    row = np.repeat(np.arange(rows, dtype=np.float32), GRID_W)
    col = np.tile(np.arange(GRID_W, dtype=np.float32), rows)
    axis_dim = HEAD_DIM // 2
    inv_freq = (np.float32(ROPE_BASE) ** (-np.arange(0, axis_dim, 2, dtype=np.float32) / np.float32(axis_dim)))
    ang_r = (row[:, None] * inv_freq[None, :]).astype(np.float32)
    ang_c = (col[:, None] * inv_freq[None, :]).astype(np.float32)
    cos = np.concatenate([np.cos(ang_r), np.cos(ang_c), np.cos(ang_r), np.cos(ang_c)], axis=1)
    sin = np.concatenate([-np.sin(ang_r), -np.sin(ang_c), np.sin(ang_r), np.sin(ang_c)], axis=1)
    cos = np.concatenate([cos, np.ones((tm, HEAD_DIM), np.float32)], axis=0)
    sin = np.concatenate([sin, np.zeros((tm, HEAD_DIM), np.float32)], axis=0)
    return jnp.asarray(cos, F32), jnp.asarray(sin, F32)


def _chan_dft_kernel(x_ref, g_ref, sh_ref, sc_ref, cs_ref, a_ref, b_ref, y_scr):
    _norm_mod_rows(x_ref, g_ref, sh_ref, sc_ref, y_scr, BF16)
    gw = cs_ref.shape[0]
    cs = cs_ref[...]
    for g in range(x_ref.shape[1] // gw):
        cols = slice(g * gw, (g + 1) * gw)
        r = jnp.dot(y_scr[:, cols], cs, preferred_element_type=F32)
        a_ref[:, cols] = r[:, :gw].astype(a_ref.dtype)
        b_ref[:, cols] = r[:, gw:].astype(b_ref.dtype)


def _dft_mats(n):
    k = np.arange(n, dtype=np.int64)
    ang = 2.0 * np.pi * ((k[:, None] * k[None, :]) % n).astype(np.float64) / n
    scale = 1.0 / math.sqrt(n)
    return np.cos(ang) * scale, np.sin(ang) * scale


def _chan_dft(x, gain, mod, tm):
    nt, d = x.shape
    gw = d // FNET_GROUPS
    c, s = _dft_mats(gw)
    cs = jnp.asarray(np.concatenate([c, s], axis=1), BF16)
    return pl.pallas_call(
        _chan_dft_kernel,
        grid=(nt // tm,),
        in_specs=[pl.BlockSpec((tm, d), lambda i: (i, 0)),
                  pl.BlockSpec((1, d), lambda i: (0, 0)),
                  mod.spec(0), mod.spec(1),
                  pl.BlockSpec((gw, 2 * gw), lambda i: (0, 0))],
        out_specs=[pl.BlockSpec((tm, d), lambda i: (i, 0)), pl.BlockSpec((tm, d), lambda i: (i, 0))],
        out_shape=[jax.ShapeDtypeStruct((nt, d), BF16), jax.ShapeDtypeStruct((nt, d), BF16)],
        scratch_shapes=[pltpu.VMEM((tm, d), BF16)],
        compiler_params=_params(("arbitrary",), 2 * tm * d * 4 + 5 * tm * d * 2 + (8 << 20)),
        name="fnet_channel_dft",
    )(x, gain.reshape(1, d), mod.table, mod.table, cs)


FFT_RADIX = 4


def _pos_fft_kernel(a_ref, b_ref, twc_ref, tws_ref, c_ref, s_ref, o_ref):
    q = a_ref.shape[0] // FFT_RADIX
    tn = a_ref.shape[1]
    ar = [a_ref[k * q:(k + 1) * q, :].astype(F32) for k in range(FFT_RADIX)]
    br = [b_ref[k * q:(k + 1) * q, :].astype(F32) for k in range(FFT_RADIX)]
    s02r, d02r, s13r, d13r = ar[0] + ar[2], ar[0] - ar[2], ar[1] + ar[3], ar[1] - ar[3]
    s02b, d02b, s13b, d13b = br[0] + br[2], br[0] - br[2], br[1] + br[3], br[1] - br[3]
    g = [(s02r + s13r, s02b + s13b), (d02r - d13b, d02b + d13r),
         (s02r - s13r, s02b - s13b), (d02r + d13b, d02b - d13r)]
    for j in range(FFT_RADIX):
        gr, gb = g[j]
        if j > 0:
            rows = slice(j * q, (j + 1) * q)
            tc = jnp.concatenate([twc_ref[rows, :]] * (tn // LANE), axis=1)
            ts = jnp.concatenate([tws_ref[rows, :]] * (tn // LANE), axis=1)
            gr, gb = gr * tc - gb * ts, gb * tc + gr * ts
        f = (jnp.dot(c_ref[...], gr.astype(BF16), preferred_element_type=F32)
             - jnp.dot(s_ref[...], gb.astype(BF16), preferred_element_type=F32))
        o_ref[j * q:(j + 1) * q, :] = f.astype(o_ref.dtype)


def _pos_fft(a, b, n_batch, seq_len):
    nt, d = a.shape
    q = seq_len // FFT_RADIX
    c, s = _dft_mats(q)
    scale = math.sqrt(q) / math.sqrt(seq_len)
    cq, sq = jnp.asarray(c * scale, BF16), jnp.asarray(s * scale, BF16)
    l2 = np.arange(q, dtype=np.int64)
    ang = np.concatenate([2.0 * np.pi * ((l2 * j) % seq_len) / seq_len for j in range(FFT_RADIX)])
    twc = jnp.asarray(np.broadcast_to(np.cos(ang)[:, None], (seq_len, LANE)), F32)
    tws = jnp.asarray(np.broadcast_to(np.sin(ang)[:, None], (seq_len, LANE)), F32)
    tn = _pow2_tile(256, d)
    return pl.pallas_call(
        _pos_fft_kernel,
        grid=(n_batch, d // tn),
        in_specs=[pl.BlockSpec((seq_len, tn), lambda bb, n: (bb, n)),
                  pl.BlockSpec((seq_len, tn), lambda bb, n: (bb, n)),
                  pl.BlockSpec((seq_len, LANE), lambda bb, n: (0, 0)),
                  pl.BlockSpec((seq_len, LANE), lambda bb, n: (0, 0)),
                  pl.BlockSpec((q, q), lambda bb, n: (0, 0)),
                  pl.BlockSpec((q, q), lambda bb, n: (0, 0))],
        out_specs=pl.BlockSpec((seq_len, tn), lambda bb, n: (bb, n)),
        out_shape=jax.ShapeDtypeStruct((nt, d), BF16),
        compiler_params=_params(("arbitrary", "arbitrary"),
                                6 * seq_len * tn * 2 + 4 * seq_len * LANE * 4 + 4 * q * q * 2
                                + 12 * seq_len * tn * 4 + (4 << 20)),
        name="fnet_position_fft",
    )(a, b, twc, tws, cq, sq)


def _resid_mm_interleave_kernel(f0_ref, f1_ref, f2_ref, f3_ref, p_ref, w_ref, x_ref, gt_ref, o_ref):
    fcat = jnp.concatenate([f[...] for f in (f0_ref, f1_ref, f2_ref, f3_ref)], axis=0)
    ftrue = jnp.dot(p_ref[...], fcat, preferred_element_type=F32).astype(BF16)
    o_ref[...] = x_ref[...] + gt_ref[0] * jnp.dot(ftrue, w_ref[...], preferred_element_type=F32)


def _resid_mm_interleave(f, w, x, mod, gate_chunk, seq_len, tm):
    n_rows, d = x.shape
    k = w.shape[0]
    q = seq_len // FFT_RADIX
    rj = tm // FFT_RADIX
    per_seq = seq_len // tm

    def f_map(j):
        return lambda i: ((i // per_seq) * (seq_len // rj) + j * (q // rj) + i % per_seq, 0)

    perm = np.zeros((tm, tm), np.float32)
    r = np.arange(rj)
    for j in range(FFT_RADIX):
        perm[FFT_RADIX * r + j, j * rj + r] = 1.0
    return pl.pallas_call(
        _resid_mm_interleave_kernel,
        grid=(n_rows // tm,),
        in_specs=[pl.BlockSpec((rj, k), f_map(j)) for j in range(FFT_RADIX)]
        + [pl.BlockSpec((tm, tm), lambda i: (0, 0)),
           pl.BlockSpec((k, d), lambda i: (0, 0)),
           pl.BlockSpec((tm, d), lambda i: (i, 0)),
           mod.spec(gate_chunk)],
        out_specs=pl.BlockSpec((tm, d), lambda i: (i, 0)),
        out_shape=jax.ShapeDtypeStruct((n_rows, d), F32),
        compiler_params=_params(("arbitrary",), 4 * tm * k * 2 + 2 * k * d * 2 + 6 * tm * d * 4 + (4 << 20)),
        name="fnet_out",
    )(f, f, f, f, jnp.asarray(perm, BF16), w, x, mod.table)


def kernel(x, c, ctx, c_ctx, ada_w, ada_b, norm_mix_g, norm_ffn_g, gm_w_in, gm_v_g, gm_w_s, gm_b_s, gm_w_out, cv_w_pw1, cv_b_pw1, cv_w_dw, cv_b_dw, cv_norm_g, cv_w_pw2, at_w_qkv, at_q_g, at_k_g, at_sink, at_w_o, ft_w_out, f_w_gate, f_w_up, f_w_down, m_w_router, m_b_router, m_w_gate, m_w_up, m_w_down):
    n_batch, seq_len, d = x.shape
    ctx_len = ctx.shape[1]
    depth = ada_w.shape[0]
    assert depth == 4 and seq_len % GRID_W == 0 and ctx_len % CHUNK == 0 and seq_len % CHUNK == 0
    assert d % (HEAD_DIM * KV_GROUP) == 0 and n_batch < 16
    n_lat, n_ctx = n_batch * seq_len, n_batch * ctx_len
    nt = n_lat + n_ctx
    tm = _pow2_tile(ROW_TILE, seq_len, n_ctx)
    tn = _pow2_tile(COL_TILE, d // KV_GROUP)
    tmd = _pow2_tile(DMA_ROWS, seq_len, n_ctx)
    tms = _pow2_tile(512, seq_len, n_ctx)
    bf = lambda w: w.astype(BF16)
    big = _cast_layer_bf16

    cc = jnp.zeros((16, d), F32).at[:n_batch].set(c).at[n_batch].set(c_ctx)
    table = _ada_table(cc, ada_w, ada_b).reshape(depth, 16 * 6, 1, d)
    mods = [_Mod(table[l], tm, seq_len, n_batch) for l in range(depth)]
    mods_s = [_Mod(table[l], tms, seq_len, n_batch) for l in range(depth)]
    mods_d = [_Mod(table[l], tmd, seq_len, n_batch) for l in range(depth)]

    x_lat, x_ctx = x.reshape(n_lat, d), ctx.reshape(n_ctx, d)

    h = _proj_call(functools.partial(_proj_gelu_kernel, n_lat // tms), (x_lat, x_ctx), norm_mix_g[0], mods_s[0], 0,
                   big(gm_w_in, 0), [], [], 2 * d, tms, tn, "gmlp_in")
    xs = _gmlp_tail(h, gm_v_g[0], gm_w_s[0], gm_b_s[0], bf(gm_w_out[0]), x_lat, x_ctx, mods_s[0], tms)
    xs = _ffn(xs, norm_ffn_g[0], mods[0], big(f_w_gate, 0), big(f_w_up, 0), big(f_w_down, 0), tm)

    w_pw1 = big(cv_w_pw1, 0)
    b_pw1 = cv_b_pw1[0].reshape(1, 2 * d)
    z = _proj_call(_proj_glu_kernel, xs, norm_mix_g[1], mods[1], 0, w_pw1,
                   [b_pw1], [pl.BlockSpec((1, 2 * d), lambda i: (0, 0))], d, tm, tn, "conv_in")
    tmc = _pow2_tile(256, seq_len, ctx_len)
    xs = _conv_tail(z, cv_w_dw[0], cv_b_dw[0], cv_norm_g[0], bf(cv_w_pw2[0]), xs,
                    _Mod(table[1], tmc, seq_len, n_batch), tmc, seq_len, n_lat, ctx_len)
    xs = _moe(xs, norm_ffn_g[1], mods[1], mods_d[1], m_w_router[0], m_b_router[0],
              big(m_w_gate, 0), big(m_w_up, 0), big(m_w_down, 0), tm, tmd)

    qkv_dim = at_w_qkv.shape[2]
    n_qk_tiles = (d + d // KV_GROUP) // tn
    cos, sin = _rope_tables(seq_len, tm)
    rope_block = lambda i: (jnp.where(i < n_lat // tm, i % (seq_len // tm), seq_len // tm), 0)
    q_gain = _rope_layout(at_q_g[0] * (HEAD_DIM ** -0.5 * LOG2E), 1)
    k_gain = _rope_layout(at_k_g[0], 1)
    gains = jnp.stack([q_gain] * (d // tn) + [k_gain] * (qkv_dim // tn - d // tn)).reshape(-1, 1, HEAD_DIM)
    n_qk = d + d // KV_GROUP
    w_qkv = jnp.concatenate([_rope_layout(at_w_qkv[0][:, :n_qk], n_qk // HEAD_DIM), at_w_qkv[0][:, n_qk:]], axis=1)
    qkv = _qkv_proj(xs, norm_mix_g[2], mods[2], bf(w_qkv), gains, cos, sin, rope_block, n_qk_tiles, tm, tn)
    o = _attention(qkv, at_sink[0], n_batch, seq_len, ctx_len, d)
    xl = _resid_mm(o, bf(at_w_o[0]), xs, mods_s[2], 2, n_lat, tms, "attn_out")
    xl = _ffn(xl, norm_ffn_g[2], mods[2], big(f_w_gate, 1), big(f_w_up, 1), big(f_w_down, 1), tm)

    a, b = _chan_dft(xl, norm_mix_g[3], mods[3], tm)
    f = _pos_fft(a, b, n_batch, seq_len)
    xl = _resid_mm_interleave(f, bf(ft_w_out[0]), xl, mods_s[3], 2, seq_len, tms)
    xl = _moe(xl, norm_ffn_g[3], mods[3], mods_d[3], m_w_router[1], m_b_router[1],
              big(m_w_gate, 1), big(m_w_up, 1), big(m_w_down, 1), tm, tmd)
    return xl.reshape(n_batch, seq_len, d)
```

```python
import functools
import math

import numpy as np
import jax
import jax.numpy as jnp
from jax import lax
from jax.experimental import pallas as pl
from jax.experimental.pallas import tpu as pltpu

F32 = jnp.float32
BF16 = jnp.bfloat16

EPS = 1e-6
GRID_W = 64
CHUNK = 128
GMLP_GROUPS = 8
CONV_WIDTH = 31
CONV_PAD = CONV_WIDTH // 2
HEAD_DIM = 128
KV_GROUP = 4
WINDOW = 128
ROPE_BASE = 10000.0
FNET_GROUPS = 8
TOP_K = 2
LOG2E = math.log2(math.e)

LANE = 128
V7X_VMEM_LIMIT = 56 << 20

ROW_TILE = 1024
COL_TILE = 512
CONV_HALO = 16
MOE_TILE = 1024
DMA_ROWS = 512
CAST_BLOCK_BYTES = 8 << 20


def _params(sem, vmem_bytes):
    return pltpu.CompilerParams(dimension_semantics=sem,
                                vmem_limit_bytes=int(min(max(vmem_bytes, 16 << 20), V7X_VMEM_LIMIT)))


def _pow2_tile(target, *sizes):
    t = target
    while any(s % t for s in sizes):
        t //= 2
    return t


def _sigmoid(x):
    return 1.0 / (1.0 + jnp.exp(-x))


def _norm_mod(x, g, shift, scale):
    ms = jnp.mean(x * x, axis=-1, keepdims=True)
    return (x * lax.rsqrt(ms + EPS) * g) * (1.0 + scale) + shift


def _norm_mod_rows(x_ref, g_ref, sh_ref, sc_ref, dst_ref, dtype, chunk=256):
    rows, d = x_ref.shape
    chunk = min(chunk, rows)
    schunk = min(4 * chunk, rows)
    gs = g_ref[...] * (1.0 + sc_ref[0])
    sh = sh_ref[0]

    def run(rs_scr):
        def stats(r, carry):
            sl = pl.ds(pl.multiple_of(r * schunk, schunk), schunk)
            xv = x_ref[sl, :]
            rs_scr[sl, :] = lax.rsqrt(jnp.sum(xv * xv, axis=-1, keepdims=True) * (1.0 / d) + EPS)
            return carry

        lax.fori_loop(0, rows // schunk, stats, 0)

        def apply(r, carry):
            sl = pl.ds(pl.multiple_of(r * chunk, chunk), chunk)
            dst_ref[sl, :] = (x_ref[sl, :] * rs_scr[sl, :] * gs + sh).astype(dtype)
            return carry

        lax.fori_loop(0, rows // chunk, apply, 0)

    pl.run_scoped(run, pltpu.VMEM((rows, 1), F32))


def _cast_kernel(x_ref, o_ref):
    o_ref[...] = x_ref[0].astype(o_ref.dtype)


def _cast_layer_bf16(w, layer):
    k, n = w.shape[-2:]
    rk = _pow2_tile(1 << int(math.log2(max(8, CAST_BLOCK_BYTES // (n * 4)))), k)
    if w.ndim == 4:
        grid = (w.shape[1], k // rk)
        in_spec = pl.BlockSpec((1, 1, rk, n), lambda e, r: (layer, e, r, 0))
        out_spec = pl.BlockSpec((1, rk, n), lambda e, r: (e, r, 0))
    else:
        grid = (k // rk,)
        in_spec = pl.BlockSpec((1, rk, n), lambda r: (layer, r, 0))
        out_spec = pl.BlockSpec((rk, n), lambda r: (r, 0))
    return pl.pallas_call(
        _cast_kernel,
        grid=grid,
        in_specs=[in_spec],
        out_specs=out_spec,
        out_shape=jax.ShapeDtypeStruct(w.shape[1:], BF16),
        compiler_params=_params(("arbitrary",) * len(grid), 4 * rk * n * 4 + (4 << 20)),
        name="cast_bf16",
    )(w)


def _ada_kernel(c_ref, w_ref, b_ref, o_ref):
    c = c_ref[...]
    s = (c * _sigmoid(c)).astype(BF16)
    o_ref[0] = jnp.dot(s, w_ref[0].astype(BF16), preferred_element_type=F32) + b_ref[0]


def _ada_table(cc, ada_w, ada_b):
    depth, d, n6 = ada_w.shape
    tn = _pow2_tile(1024, n6)
    rows = cc.shape[0]
    return pl.pallas_call(
        _ada_kernel,
        grid=(depth, n6 // tn),
        in_specs=[pl.BlockSpec((rows, d), lambda l, j: (0, 0)),
                  pl.BlockSpec((1, d, tn), lambda l, j: (l, 0, j)),
                  pl.BlockSpec((1, 1, tn), lambda l, j: (l, 0, j))],
        out_specs=pl.BlockSpec((1, rows, tn), lambda l, j: (l, 0, j)),
        out_shape=jax.ShapeDtypeStruct((depth, rows, n6), F32),
        compiler_params=_params(("arbitrary", "arbitrary"), 2 * d * tn * 4 + d * tn * 2 + (8 << 20)),
        name="ada_table",
    )(cc, ada_w, ada_b.reshape(depth, 1, n6))


class _Mod:
    def __init__(self, table, tm, seq_len, n_batch):
        self.table, self.tm, self.seq_len, self.n_batch = table, tm, seq_len, n_batch
        self.d = table.shape[-1]

    def spec(self, chunk):
        tm, seq_len, n_batch = self.tm, self.seq_len, self.n_batch

        def index(i, *_):
            return (jnp.minimum(i * tm // seq_len, n_batch) * 6 + chunk, 0, 0)

        return pl.BlockSpec((1, 1, self.d), index)


def _gelu_tanh(x):
    return 0.5 * x * (1.0 + jnp.tanh(math.sqrt(2.0 / math.pi) * (x + 0.044715 * (x * x * x))))


def _proj_gelu_kernel(n_lat_blocks, xl_ref, xc_ref, g_ref, sh_ref, sc_ref, w_ref, o_ref, y_scr):
    i = pl.program_id(0)

    @pl.when(i < n_lat_blocks)
    def _():
        _norm_mod_rows(xl_ref, g_ref, sh_ref, sc_ref, y_scr, BF16)

    @pl.when(i >= n_lat_blocks)
    def _():
        _norm_mod_rows(xc_ref, g_ref, sh_ref, sc_ref, y_scr, BF16)

    tn = w_ref.shape[2]
    for j in range(w_ref.shape[0]):
        acc = jnp.dot(y_scr[...], w_ref[j], preferred_element_type=F32)
        o_ref[:, j * tn:(j + 1) * tn] = _gelu_tanh(acc).astype(o_ref.dtype)


def _proj_glu_kernel(x_ref, g_ref, sh_ref, sc_ref, w_ref, b_ref, o_ref, y_scr):
    _norm_mod_rows(x_ref, g_ref, sh_ref, sc_ref, y_scr, BF16)
    tn = w_ref.shape[2]
    half = w_ref.shape[0] // 2
    for j in range(half):
        y = y_scr[...]
        a = jnp.dot(y, w_ref[j], preferred_element_type=F32) + b_ref[:, j * tn:(j + 1) * tn]
        gate = (jnp.dot(y, w_ref[j + half], preferred_element_type=F32)
                + b_ref[:, (j + half) * tn:(j + half + 1) * tn])
        o_ref[:, j * tn:(j + 1) * tn] = (a * _sigmoid(gate)).astype(o_ref.dtype)


def _rope_layout(t, n_heads):
    lead = t.shape[:-1]
    q = HEAD_DIM // 4
    t = t.reshape(*lead, n_heads, 2, 2, q)
    return jnp.swapaxes(t, -2, -3).reshape(*lead, n_heads * HEAD_DIM)


def _rope_partner(x):
    return pltpu.roll(x, HEAD_DIM // 2, 1)


def _qkv_kernel(n_qk_tiles, x_ref, g_ref, sh_ref, sc_ref, w_ref, hg_ref, cos_ref, sin_ref, o_ref, y_scr):
    j = pl.program_id(1)

    @pl.when(j == 0)
    def _():
        _norm_mod_rows(x_ref, g_ref, sh_ref, sc_ref, y_scr, BF16)

    acc = jnp.dot(y_scr[...], w_ref[j], preferred_element_type=F32)

    @pl.when(j < n_qk_tiles)
    def _():
        hg, cos, sin = hg_ref[0], cos_ref[...], sin_ref[...]
        for h in range(acc.shape[1] // HEAD_DIM):
            t = acc[:, h * HEAD_DIM:(h + 1) * HEAD_DIM]
            t = t * lax.rsqrt(jnp.mean(t * t, axis=-1, keepdims=True) + EPS) * hg
            t = t * cos + _rope_partner(t) * sin
            o_ref[:, h * HEAD_DIM:(h + 1) * HEAD_DIM] = t.astype(o_ref.dtype)

    @pl.when(j >= n_qk_tiles)
    def _():
        o_ref[...] = acc.astype(o_ref.dtype)


def _qkv_proj(x, gain, mod, w, gains, cos, sin, rope_block, n_qk_tiles, tm, tn):
    nt, d = x.shape
    n_tiles = w.shape[1] // tn
    w_tiles = jnp.swapaxes(w.reshape(d, n_tiles, tn), 0, 1)
    return pl.pallas_call(
        functools.partial(_qkv_kernel, n_qk_tiles),
        grid=(nt // tm, n_tiles),
        in_specs=[pl.BlockSpec((tm, d), lambda i, j: (i, 0)),
                  pl.BlockSpec((1, d), lambda i, j: (0, 0)),
                  mod.spec(0), mod.spec(1),
                  pl.BlockSpec((n_tiles, d, tn), lambda i, j: (0, 0, 0), pipeline_mode=pl.Buffered(1)),
                  pl.BlockSpec((1, 1, HEAD_DIM), lambda i, j: (j, 0, 0)),
                  pl.BlockSpec((tm, HEAD_DIM), lambda i, j: rope_block(i)),
                  pl.BlockSpec((tm, HEAD_DIM), lambda i, j: rope_block(i))],
        out_specs=pl.BlockSpec((None, tm, tn), lambda i, j: (j, i, 0)),
        out_shape=jax.ShapeDtypeStruct((n_tiles, nt, tn), BF16),
        scratch_shapes=[pltpu.VMEM((tm, d), BF16)],
        compiler_params=_params(("arbitrary", "arbitrary"),
                                2 * tm * d * 4 + tm * d * 2 + d * n_tiles * tn * 2 + 10 * tm * tn * 4 + (6 << 20)),
        name="qkv",
    )(x, gain.reshape(1, d), mod.table, mod.table, w_tiles, gains, cos, sin)


def _proj_call(kernel, x, gain, mod, chunk0, w, extra_in, extra_specs, n_out, tm, tn, name):
    if isinstance(x, tuple):
        x_lat, x_ctx = x
        nlb = x_lat.shape[0] // tm
        nt, d = x_lat.shape[0] + x_ctx.shape[0], x_lat.shape[1]
        in_specs = [pl.BlockSpec((tm, d), lambda i: (jnp.minimum(i, nlb - 1), 0)),
                    pl.BlockSpec((tm, d), lambda i: (jnp.maximum(i - nlb, 0), 0))]
        args = [x_lat, x_ctx]
    else:
        nt, d = x.shape
        in_specs = [pl.BlockSpec((tm, d), lambda i: (i, 0))]
        args = [x]
    in_specs += [pl.BlockSpec((1, d), lambda i: (0, 0)), mod.spec(chunk0), mod.spec(chunk0 + 1)]
    args += [gain.reshape(1, d), mod.table, mod.table]
    n_x = len(args)
    n_w = w.shape[1]
    w_tiles = jnp.swapaxes(w.reshape(d, n_w // tn, tn), 0, 1)
    in_specs.append(pl.BlockSpec((n_w // tn, d, tn), lambda i: (0, 0, 0), pipeline_mode=pl.Buffered(1)))
    args.append(w_tiles)
    in_specs += extra_specs
    args += extra_in
    vmem = 2 * n_x * tm * d * 4 + tm * d * 2 + d * n_w * 2 + 2 * tm * n_out * 2 + 8 * tm * tn * 4
    return pl.pallas_call(
        kernel,
        grid=(nt // tm,),
        in_specs=in_specs,
        out_specs=pl.BlockSpec((tm, n_out), lambda i: (i, 0)),
        out_shape=jax.ShapeDtypeStruct((nt, n_out), BF16),
        scratch_shapes=[pltpu.VMEM((tm, d), BF16)],
        compiler_params=_params(("arbitrary",), vmem + (4 << 20)),
        name=name,
    )(*args)


def _resid_mm_kernel(a_ref, w_ref, x_ref, gt_ref, o_ref):
    o_ref[...] = x_ref[...] + gt_ref[0] * jnp.dot(a_ref[...], w_ref[...], preferred_element_type=F32)


def _resid_mm(a, w, x, mod, gate_chunk, n_rows, tm, name):
    k, d = w.shape
    return pl.pallas_call(
        _resid_mm_kernel,
        grid=(n_rows // tm,),
        in_specs=[pl.BlockSpec((tm, k), lambda i: (i, 0)),
                  pl.BlockSpec((k, d), lambda i: (0, 0)),
                  pl.BlockSpec((tm, d), lambda i: (i, 0)),
                  mod.spec(gate_chunk)],
        out_specs=pl.BlockSpec((tm, d), lambda i: (i, 0)),
        out_shape=jax.ShapeDtypeStruct((n_rows, d), F32),
        compiler_params=_params(("arbitrary",), 2 * tm * k * 2 + 2 * k * d * 2 + 5 * tm * d * 4 + (4 << 20)),
        name=name,
    )(a, w, x, mod.table)


def _gmlp_tail_kernel(n_lat_blocks, h_ref, vg_ref, ws_ref, bs_ref, w_ref, xl_ref, xc_ref, gt_ref, o_ref, z_scr):
    tm, d = xl_ref.shape
    gw = d // GMLP_GROUPS
    v = h_ref[:, d:].astype(F32)
    vn = (v * lax.rsqrt(jnp.mean(v * v, axis=-1, keepdims=True) + EPS) * vg_ref[...]).astype(BF16)
    for c in range(tm // CHUNK):
        rows = slice(c * CHUNK, (c + 1) * CHUNK)
        for g in range(GMLP_GROUPS):
            cols = slice(g * gw, (g + 1) * gw)
            sv = jnp.dot(ws_ref[g], vn[rows, cols], preferred_element_type=F32) + bs_ref[g]
            z_scr[rows, cols] = (h_ref[rows, cols].astype(F32) * sv).astype(BF16)
    x = jnp.where(pl.program_id(0) < n_lat_blocks, xl_ref[...], xc_ref[...])
    o_ref[...] = x + gt_ref[0] * jnp.dot(z_scr[...], w_ref[...], preferred_element_type=F32)


def _gmlp_tail(h, v_g, w_s, b_s, w_out, x_lat, x_ctx, mod, tm):
    d = x_lat.shape[1]
    nt = x_lat.shape[0] + x_ctx.shape[0]
    nlb = x_lat.shape[0] // tm
    bs_tile = jnp.broadcast_to(b_s[:, :, None], (GMLP_GROUPS, CHUNK, d // GMLP_GROUPS)).astype(F32)
    return pl.pallas_call(
        functools.partial(_gmlp_tail_kernel, nlb),
        grid=(nt // tm,),
        in_specs=[pl.BlockSpec((tm, 2 * d), lambda i: (i, 0)),
                  pl.BlockSpec((1, d), lambda i: (0, 0)),
                  pl.BlockSpec((GMLP_GROUPS, CHUNK, CHUNK), lambda i: (0, 0, 0)),
                  pl.BlockSpec((GMLP_GROUPS, CHUNK, d // GMLP_GROUPS), lambda i: (0, 0, 0)),
                  pl.BlockSpec((d, d), lambda i: (0, 0)),
                  pl.BlockSpec((tm, d), lambda i: (jnp.minimum(i, nlb - 1), 0)),
                  pl.BlockSpec((tm, d), lambda i: (jnp.maximum(i - nlb, 0), 0)),
                  mod.spec(2)],
        out_specs=pl.BlockSpec((tm, d), lambda i: (i, 0)),
        out_shape=jax.ShapeDtypeStruct((nt, d), F32),
        scratch_shapes=[pltpu.VMEM((tm, d), BF16)],
        compiler_params=_params(("arbitrary",), 2 * tm * 2 * d * 2 + 2 * d * d * 2 + 8 * tm * d * 4 + (6 << 20)),
        name="gmlp_tail",
    )(h, v_g.reshape(1, d), w_s.astype(BF16), bs_tile, w_out, x_lat, x_ctx, mod.table)


def _conv_tail_kernel(edges, z_ref, zp_ref, zn_ref, wdw_ref, bdw_ref, ng_ref, w_ref, x_ref, gt_ref,
                      o_ref, ext_scr, cv_scr, sh_scr):
    i = pl.program_id(0)
    tm, d = x_ref.shape
    first, last = edges(i)
    ext_scr[0:CONV_HALO, :] = jnp.where(first, 0.0, zp_ref[...].astype(F32))
    ext_scr[CONV_HALO:CONV_HALO + tm, :] = z_ref[...].astype(F32)
    ext_scr[CONV_HALO + tm:, :] = jnp.where(last, 0.0, zn_ref[...].astype(F32))

    rc = min(64, tm)
    cc = sh_scr.shape[2]
    sh_rows = sh_scr.shape[1]

    def col_body(c, carry):
        cols = pl.ds(pl.multiple_of(c * cc, cc), cc)
        wts = wdw_ref[:, cols]
        bias = bdw_ref[:, cols]
        for b in range(1, 8):
            sh_scr[b] = ext_scr[b:b + sh_rows, cols]
        for r in range(tm // rc):
            acc = jnp.zeros((rc, cc), F32) + bias
            for k in range(CONV_WIDTH):
                off = r * rc + CONV_HALO - CONV_PAD + k
                b, a = off % 8, off - off % 8
                src = ext_scr[a:a + rc, cols] if b == 0 else sh_scr[b, a:a + rc, :]
                acc = acc + wts[k:k + 1, :] * src
            cv_scr[r * rc:(r + 1) * rc, cols] = acc
        return carry

    lax.fori_loop(0, d // cc, col_body, 0)
    cv = cv_scr[...]
    t = cv * lax.rsqrt(jnp.mean(cv * cv, axis=-1, keepdims=True) + EPS) * ng_ref[...]
    t = (t * _sigmoid(t)).astype(BF16)
    o_ref[...] = x_ref[...] + gt_ref[0] * jnp.dot(t, w_ref[...], preferred_element_type=F32)


def _conv_tail(z, w_dw, b_dw, n_g, w_pw2, x, mod, tm, seq_len, n_lat, ctx_len):
    nt, d = x.shape
    hb = tm // CONV_HALO
    n_halo_blocks = nt // CONV_HALO
    lat_blocks, seq_blocks, ctx_blocks = n_lat // tm, seq_len // tm, ctx_len // tm

    def edges(i):
        in_lat = i < lat_blocks
        pos = jnp.where(in_lat, i % seq_blocks, (i - lat_blocks) % ctx_blocks)
        per = jnp.where(in_lat, seq_blocks, ctx_blocks)
        return pos == 0, pos == per - 1

    kdw = w_dw.shape[0]
    kpad = -(-kdw // 8) * 8
    w_dw_p = jnp.zeros((kpad, d), F32).at[:kdw].set(w_dw)
    return pl.pallas_call(
        functools.partial(_conv_tail_kernel, edges),
        grid=(nt // tm,),
        in_specs=[pl.BlockSpec((tm, d), lambda i: (i, 0)),
                  pl.BlockSpec((CONV_HALO, d), lambda i: (jnp.maximum(i * hb - 1, 0), 0)),
                  pl.BlockSpec((CONV_HALO, d), lambda i: (jnp.minimum((i + 1) * hb, n_halo_blocks - 1), 0)),
                  pl.BlockSpec((kpad, d), lambda i: (0, 0)),
                  pl.BlockSpec((1, d), lambda i: (0, 0)),
                  pl.BlockSpec((1, d), lambda i: (0, 0)),
                  pl.BlockSpec((d, d), lambda i: (0, 0)),
                  pl.BlockSpec((tm, d), lambda i: (i, 0)),
                  mod.spec(2)],
        out_specs=pl.BlockSpec((tm, d), lambda i: (i, 0)),
        out_shape=jax.ShapeDtypeStruct((nt, d), F32),
        scratch_shapes=[pltpu.VMEM((tm + 2 * CONV_HALO, d), F32), pltpu.VMEM((tm, d), F32),
                        pltpu.VMEM((8, tm + 2 * CONV_HALO - 8, min(256, d)), F32)],
        compiler_params=_params(("arbitrary",), 2 * d * d * 2 + 12 * tm * d * 4 + (6 << 20)),
        name="conv_tail",
    )(z, z, z, w_dw_p, b_dw.reshape(1, d), n_g.reshape(1, d), w_pw2, x, mod.table)


def _copy_rows(src_ref, dst_ref, chunk=256):
    rows = src_ref.shape[0]
    chunk = min(chunk, rows)

    def body(r, carry):
        sl = pl.ds(pl.multiple_of(r * chunk, chunk), chunk)
        dst_ref[sl, :] = src_ref[sl, :].astype(dst_ref.dtype)
        return carry

    lax.fori_loop(0, rows // chunk, body, 0)


def _swiglu_partial(y, wg, wu, wd):
    hg = jnp.dot(y, wg, preferred_element_type=F32)
    hu = jnp.dot(y, wu, preferred_element_type=F32)
    h = (hg * _sigmoid(hg) * hu).astype(BF16)
    return jnp.dot(h, wd, preferred_element_type=F32)


def _ffn_kernel(x_ref, g_ref, sh_ref, sc_ref, gt_ref, wg_ref, wu_ref, wd_ref, o_ref, y_scr):
    f = pl.program_id(1)

    @pl.when(f == 0)
    def _():
        _norm_mod_rows(x_ref, g_ref, sh_ref, sc_ref, y_scr, BF16)
        o_ref[...] = x_ref[...] + gt_ref[0] * _swiglu_partial(y_scr[...], wg_ref[...], wu_ref[...], wd_ref[...])

    @pl.when(f > 0)
    def _():
        o_ref[...] += gt_ref[0] * _swiglu_partial(y_scr[...], wg_ref[...], wu_ref[...], wd_ref[...])


def _ffn(x, gain, mod, w_gate, w_up, w_down, tm):
    nt, d = x.shape
    dff = w_gate.shape[1]
    tf = 512 if dff % 512 == 0 else 256
    while dff % tf:
        tf //= 2
    vmem = 4 * tm * d * 4 + tm * d * 2 + 6 * d * tf * 2 + 5 * tm * tf * 4
    return pl.pallas_call(
        _ffn_kernel,
        grid=(nt // tm, dff // tf),
        in_specs=[pl.BlockSpec((tm, d), lambda i, f: (i, 0)),
                  pl.BlockSpec((1, d), lambda i, f: (0, 0)),
                  mod.spec(3), mod.spec(4), mod.spec(5),
                  pl.BlockSpec((d, tf), lambda i, f: (0, f)),
                  pl.BlockSpec((d, tf), lambda i, f: (0, f)),
                  pl.BlockSpec((tf, d), lambda i, f: (f, 0))],
        out_specs=pl.BlockSpec((tm, d), lambda i, f: (i, 0)),
        out_shape=jax.ShapeDtypeStruct((nt, d), F32),
        scratch_shapes=[pltpu.VMEM((tm, d), BF16)],
        compiler_params=_params(("arbitrary", "arbitrary"), vmem + (4 << 20)),
        name="ffn_dense",
    )(x, gain.reshape(1, d), mod.table, mod.table, mod.table, w_gate, w_up, w_down)


def _route_kernel(n_experts, x_ref, g_ref, sh_ref, sc_ref, wh_ref, wl_ref, br_ref, tri_ref,
                  y_ref, info_ref, cnt_ref, carry):
    i = pl.program_id(0)

    @pl.when(i == 0)
    def _():
        carry[...] = jnp.zeros_like(carry)

    _norm_mod_rows(x_ref, g_ref, sh_ref, sc_ref, y_ref, F32)
    y = y_ref[...]
    yh = y.astype(BF16)
    yl = (y - yh.astype(F32)).astype(BF16)
    both = jnp.dot(yh, wl_ref[...], preferred_element_type=F32)
    lg = both[:, :LANE] + both[:, LANE:] + jnp.dot(yl, wh_ref[...], preferred_element_type=F32) + br_ref[...]
    lane = lax.broadcasted_iota(jnp.int32, lg.shape, 1)
    neg = jnp.float32(-jnp.inf)
    lg = jnp.where(lane < n_experts, lg, neg)
    m1 = jnp.max(lg, axis=-1, keepdims=True)
    i1 = jnp.min(jnp.where(lg == m1, lane, LANE), axis=-1, keepdims=True)
    lg2 = jnp.where(lane == i1, neg, lg)
    m2 = jnp.max(lg2, axis=-1, keepdims=True)
    i2 = jnp.min(jnp.where(lg2 == m2, lane, LANE), axis=-1, keepdims=True)
    e = jnp.exp(m2 - m1)
    w1 = 1.0 / (1.0 + e)
    w2 = e * w1
    hit1, hit2 = lane == i1, lane == i2
    onehot = (hit1 | hit2).astype(F32)
    pre = jnp.dot(tri_ref[...], onehot.astype(BF16), preferred_element_type=F32) + carry[0:1, :]
    r1 = jnp.sum(jnp.where(hit1, pre, 0.0), axis=-1, keepdims=True)
    r2 = jnp.sum(jnp.where(hit2, pre, 0.0), axis=-1, keepdims=True)
    total = carry[0:1, :] + jnp.sum(onehot, axis=0, keepdims=True)
    carry[...] = jnp.broadcast_to(total, carry.shape)
    cnt_ref[...] = jnp.broadcast_to(total, cnt_ref.shape)
    info = jnp.where(lane == 0, i1.astype(F32), 0.0)
    info = jnp.where(lane == 1, i2.astype(F32), info)
    info = jnp.where(lane == 2, w1, info)
    info = jnp.where(lane == 3, w2, info)
    info = jnp.where(lane == 4, r1, info)
    info = jnp.where(lane == 5, r2, info)
    info_ref[...] = info


def _route(x, gain, mod, w_router, b_router, tm):
    nt, d = x.shape
    n_experts = w_router.shape[1]
    wr = jnp.zeros((d, LANE), F32).at[:, :n_experts].set(w_router)
    wr_hi = wr.astype(BF16)
    wr_lo = jnp.concatenate([wr_hi, (wr - wr_hi.astype(F32)).astype(BF16)], axis=1)
    br = jnp.zeros((1, LANE), F32).at[0, :n_experts].set(b_router)
    tri = jnp.tril(jnp.ones((tm, tm), BF16), -1)
    return pl.pallas_call(
        functools.partial(_route_kernel, n_experts),
        grid=(nt // tm,),
        in_specs=[pl.BlockSpec((tm, d), lambda i: (i, 0)),
                  pl.BlockSpec((1, d), lambda i: (0, 0)),
                  mod.spec(3), mod.spec(4),
                  pl.BlockSpec((d, LANE), lambda i: (0, 0)),
                  pl.BlockSpec((d, 2 * LANE), lambda i: (0, 0)),
                  pl.BlockSpec((1, LANE), lambda i: (0, 0)),
                  pl.BlockSpec((tm, tm), lambda i: (0, 0))],
        out_specs=[pl.BlockSpec((tm, d), lambda i: (i, 0)),
                   pl.BlockSpec((tm, LANE), lambda i: (i, 0)),
                   pl.BlockSpec((8, LANE), lambda i: (0, 0))],
        out_shape=[jax.ShapeDtypeStruct((nt, d), F32),
                   jax.ShapeDtypeStruct((nt, LANE), F32),
                   jax.ShapeDtypeStruct((8, LANE), F32)],
        scratch_shapes=[pltpu.VMEM((8, LANE), F32)],
        compiler_params=_params(("arbitrary",), 4 * tm * d * 4 + 2 * tm * d * 2 + 2 * tm * tm * 2 + (8 << 20)),
        name="moe_route",
    )(x, gain.reshape(1, d), mod.table, mod.table, wr_hi, wr_lo, br, tri)


def _expert_kernel(n_f, blk_ref, nused_ref, rowtok_ref, y_hbm, wg_ref, wu_ref, wd_ref, o_ref, xbuf, y_scr, sems):
    del blk_ref
    i, f = pl.program_id(0), pl.program_id(1)
    tm = o_ref.shape[0]
    chunk = tm // n_f
    n_used = nused_ref[0]
    used = i < n_used
    slot = i % 2
    nxt = jnp.minimum(i + 1, n_used - 1)

    def row_copy(blk, slot_, row):
        tok = rowtok_ref[blk * tm + row]
        return pltpu.make_async_copy(y_hbm.at[pl.ds(tok, 1), :], xbuf.at[slot_, pl.ds(row, 1), :], sems.at[slot_])

    def copy_rows(blk, slot_, first_row, n, start):
        def body(r, carry):
            cp = row_copy(blk, slot_, first_row + r)
            cp.start() if start else cp.wait()
            return carry

        lax.fori_loop(0, n, body, 0, unroll=1 if start else 8)

    @pl.when(jnp.logical_and(i == 0, f == 0))
    def _():
        copy_rows(0, 0, 0, tm, True)
        copy_rows(0, 0, 0, tm, False)

    def start_chunk():
        for r in range(chunk):
            row_copy(nxt, 1 - slot, f * chunk + r).start()

    lag = min(2, n_f)

    @pl.when(jnp.logical_and(used, f == 0))
    def _():
        @pl.when(i > 0)
        def _():
            copy_rows(i, slot, (n_f - lag) * chunk, lag * chunk, False)

        _copy_rows(xbuf.at[slot], y_scr)
        o_ref[...] = _swiglu_partial(y_scr[...], wg_ref[0], wu_ref[0], wd_ref[0])
        start_chunk()

    @pl.when(jnp.logical_and(used, f > 0))
    def _():
        @pl.when(f >= lag)
        def _():
            for r in range(chunk):
                row_copy(nxt, 1 - slot, (f - lag) * chunk + r).wait()

        o_ref[...] += _swiglu_partial(y_scr[...], wg_ref[0], wu_ref[0], wd_ref[0])
        start_chunk()

    @pl.when(jnp.logical_and(i == n_used - 1, f == n_f - 1))
    def _():
        copy_rows(nxt, 1 - slot, (n_f - lag) * chunk, lag * chunk, False)

    @pl.when(jnp.logical_and(jnp.logical_not(used), f == 0))
    def _():
        o_ref[...] = jnp.zeros_like(o_ref)


def _experts(y, row_tok, block_e, n_used, w_gate, w_up, w_down, tm):
    n_rows = row_tok.shape[0]
    d = y.shape[1]
    dff = w_gate.shape[2]
    tf = _pow2_tile(512, dff)
    n_f = dff // tf
    assert tm % n_f == 0
    vmem = 4 * tm * d * 4 + tm * d * 2 + 6 * d * tf * 2 + 5 * tm * tf * 4
    grid_spec = pltpu.PrefetchScalarGridSpec(
        num_scalar_prefetch=3,
        grid=(n_rows // tm, n_f),
        in_specs=[pl.BlockSpec(memory_space=pl.ANY),
                  pl.BlockSpec((1, d, tf), lambda i, f, be, nu, rt: (be[i], 0, f)),
                  pl.BlockSpec((1, d, tf), lambda i, f, be, nu, rt: (be[i], 0, f)),
                  pl.BlockSpec((1, tf, d), lambda i, f, be, nu, rt: (be[i], f, 0))],
        out_specs=pl.BlockSpec((tm, d), lambda i, f, be, nu, rt: (i, 0)),
        scratch_shapes=[pltpu.VMEM((2, tm, d), F32), pltpu.VMEM((tm, d), BF16), pltpu.SemaphoreType.DMA((2,))],
    )
    return pl.pallas_call(
        functools.partial(_expert_kernel, n_f),
        grid_spec=grid_spec,
        out_shape=jax.ShapeDtypeStruct((n_rows, d), F32),
        compiler_params=_params(("arbitrary", "arbitrary"), vmem + (4 << 20)),
        name="moe_experts",
    )(block_e, n_used, row_tok, y, w_gate, w_up, w_down)


def _combine_kernel(dest_hbm, ys_hbm, x_ref, gt_ref, info_ref, o_ref, idx_smem, buf, sem_idx, sem_rows):
    i = pl.program_id(0)
    rows = x_ref.shape[0]

    def row_copy(slot, t, s):
        d = idx_smem[slot, TOP_K * t + s]
        return pltpu.make_async_copy(ys_hbm.at[pl.ds(d, 1), :], buf.at[slot, s, pl.ds(t, 1), :], sem_rows.at[slot])

    def fetch(step, slot):
        idx_copy = pltpu.make_async_copy(dest_hbm.at[step], idx_smem.at[slot], sem_idx)
        idx_copy.start()
        idx_copy.wait()

        def issue(t, carry):
            for s in range(TOP_K):
                row_copy(slot, t, s).start()
            return carry

        lax.fori_loop(0, rows, issue, 0, unroll=4)

    def step(slot):
        @pl.when(i + 1 < pl.num_programs(0))
        def _():
            fetch(i + 1, 1 - slot)

        def drain(t, carry):
            for s in range(TOP_K):
                row_copy(slot, t, s).wait()
            return carry

        lax.fori_loop(0, rows, drain, 0, unroll=8)
        info = info_ref[...]
        mix = info[:, 2:3] * buf[slot, 0] + info[:, 3:4] * buf[slot, 1]
        o_ref[...] = x_ref[...] + gt_ref[0] * mix

    @pl.when(i == 0)
    def _():
        fetch(0, 0)

    for slot in range(2):
        pl.when(i % 2 == slot)(functools.partial(step, slot))


def _combine(ys, dest, x, mod, info, tmd):
    nt, d = x.shape
    return pl.pallas_call(
        _combine_kernel,
        grid=(nt // tmd,),
        in_specs=[pl.BlockSpec(memory_space=pl.ANY),
                  pl.BlockSpec(memory_space=pl.ANY),
                  pl.BlockSpec((tmd, d), lambda i: (i, 0)),
                  mod.spec(5),
                  pl.BlockSpec((tmd, LANE), lambda i: (i, 0))],
        out_specs=pl.BlockSpec((tmd, d), lambda i: (i, 0)),
        out_shape=jax.ShapeDtypeStruct((nt, d), F32),
        scratch_shapes=[pltpu.SMEM((2, TOP_K * tmd), jnp.int32),
                        pltpu.VMEM((2, TOP_K, tmd, d), F32),
                        pltpu.SemaphoreType.DMA(()), pltpu.SemaphoreType.DMA((2,))],
        compiler_params=_params(("arbitrary",), 12 * tmd * d * 4 + (8 << 20)),
        name="moe_combine",
    )(dest.reshape(nt // tmd, TOP_K * tmd), ys, x, mod.table, info)


def _moe(x, gain, mod_route, mod_comb, w_router, b_router, w_gate, w_up, w_down, tm, tmd):
    nt, d = x.shape
    n_experts = w_router.shape[1]
    y, info, cnt = _route(x, gain, mod_route, w_router, b_router, tm)
    counts = cnt[0, :n_experts].astype(jnp.int32)
    padded = (counts + MOE_TILE - 1) // MOE_TILE * MOE_TILE
    pend = jnp.cumsum(padded)
    pstart = pend - padded
    n_rows = (-(-(nt * TOP_K) // MOE_TILE) + n_experts) * MOE_TILE
    nb = n_rows // MOE_TILE
    e_idx = info[:, 0:TOP_K].astype(jnp.int32)
    rank = info[:, 4:4 + TOP_K].astype(jnp.int32)
    dest = (pstart[e_idx] + rank).reshape(-1)
    block_e = jnp.minimum(jnp.searchsorted(pend, jnp.arange(nb, dtype=jnp.int32) * MOE_TILE, side='right'),
                          n_experts - 1).astype(jnp.int32)
    n_used = (pend[-1:] // MOE_TILE).astype(jnp.int32)
    tok = jnp.repeat(jnp.arange(nt, dtype=jnp.int32), TOP_K)
    row_tok = jnp.zeros((n_rows,), jnp.int32).at[dest].set(tok, unique_indices=True)
    ys = _experts(y, row_tok, block_e, n_used, w_gate, w_up, w_down, MOE_TILE)
    return _combine(ys, dest, x, mod_comb, info, tmd)


def _attn_kernel(seq_len, n_q_tiles, sink_ref, *refs):
    q_refs = refs[:n_q_tiles]
    kp_ref, km_ref, kn_ref, kc_ref, vp_ref, vm_ref, vn_ref, vc_ref, o_ref = refs[n_q_tiles:]
    i = pl.program_id(1)
    tq = o_ref.shape[0]
    tile_w = km_ref.shape[1]
    n_kv = tile_w // HEAD_DIM
    n_ctx = kc_ref.shape[0]
    span = tq + 2 * WINDOW
    q_pos = i * tq + lax.broadcasted_iota(jnp.int32, (tq, span), 0)
    k_pos = i * tq - WINDOW + lax.broadcasted_iota(jnp.int32, (tq, span), 1)
    ok = (jnp.abs(q_pos - k_pos) <= WINDOW) & (k_pos >= 0) & (k_pos < seq_len)
    bias = jnp.where(ok, 0.0, -1e30).astype(F32)
    ones_b = jnp.ones((span, HEAD_DIM), BF16)
    ones_c = jnp.ones((n_ctx, HEAD_DIM), BF16)
    nt_dims = (((1,), (1,)), ((), ()))
    for kh in range(n_kv):
        hs = slice(kh * HEAD_DIM, (kh + 1) * HEAD_DIM)
        k_band = jnp.concatenate([kp_ref[:, hs], km_ref[:, hs], kn_ref[:, hs]], axis=0)
        k_ctx = kc_ref[:, hs]
        v_band = jnp.concatenate([jnp.concatenate([vp_ref[:, hs], vm_ref[:, hs], vn_ref[:, hs]], axis=0), ones_b],
                                 axis=1)
        v_ctx = jnp.concatenate([vc_ref[:, hs], ones_c], axis=1)
        for g in range(KV_GROUP):
            h = kh * KV_GROUP + g
            qs = slice(h * HEAD_DIM, (h + 1) * HEAD_DIM)
            lane0 = h * HEAD_DIM
            q = q_refs[lane0 // tile_w][:, lane0 % tile_w:lane0 % tile_w + HEAD_DIM]
            s_b = lax.dot_general(q, k_band, nt_dims, preferred_element_type=F32) + bias
            s_c = lax.dot_general(q, k_ctx, nt_dims, preferred_element_type=F32)
            sink = sink_ref[h] * LOG2E
            m = jnp.maximum(jnp.maximum(jnp.max(s_b, axis=-1, keepdims=True),
                                        jnp.max(s_c, axis=-1, keepdims=True)), sink)
            p_b = jnp.exp2(s_b - m).astype(BF16)
            p_c = jnp.exp2(s_c - m).astype(BF16)
            acc = (jnp.dot(p_b, v_band, preferred_element_type=F32)
                   + jnp.dot(p_c, v_ctx, preferred_element_type=F32))
            denom = acc[:, HEAD_DIM:HEAD_DIM + 1] + jnp.exp2(sink - m)
            o_ref[:, qs] = (acc[:, :HEAD_DIM] / denom).astype(o_ref.dtype)


def _attention(qkv, sink, n_batch, seq_len, ctx_len, d):
    tq = _pow2_tile(256, seq_len)
    kvw = qkv.shape[2]
    n_q_tiles = d // kvw
    assert kvw == d // KV_GROUP and qkv.shape[0] == n_q_tiles + 2
    wb = WINDOW
    qb, sb = seq_len // tq, seq_len // wb
    n_lat = n_batch * seq_len
    ctx_block0 = n_lat // ctx_len

    def prev_map(tile):
        return lambda b, i, s: (tile, b * sb + jnp.maximum(i * (tq // wb) - 1, 0), 0)

    def main_map(tile):
        return lambda b, i, s: (tile, b * qb + i, 0)

    def next_map(tile):
        return lambda b, i, s: (tile, b * sb + jnp.minimum((i + 1) * (tq // wb), sb - 1), 0)

    def ctx_map(tile):
        return lambda b, i, s: (tile, ctx_block0 + b, 0)

    q_specs = [pl.BlockSpec((None, tq, kvw), main_map(t)) for t in range(n_q_tiles)]
    kv_specs = []
    for tile in (n_q_tiles, n_q_tiles + 1):
        kv_specs += [pl.BlockSpec((None, wb, kvw), prev_map(tile)), pl.BlockSpec((None, tq, kvw), main_map(tile)),
                     pl.BlockSpec((None, wb, kvw), next_map(tile)), pl.BlockSpec((None, ctx_len, kvw), ctx_map(tile))]
    grid_spec = pltpu.PrefetchScalarGridSpec(
        num_scalar_prefetch=1,
        grid=(n_batch, qb),
        in_specs=q_specs + kv_specs,
        out_specs=pl.BlockSpec((tq, d), lambda b, i, s: (b * qb + i, 0)),
    )
    return pl.pallas_call(
        functools.partial(_attn_kernel, seq_len, n_q_tiles),
        grid_spec=grid_spec,
        out_shape=jax.ShapeDtypeStruct((n_lat, d), BF16),
        compiler_params=_params(("arbitrary", "arbitrary"), 32 << 20),
        name="attention",
    )(sink, *([qkv] * (n_q_tiles + 8)))


def _rope_tables(seq_len, tm):
    rows = seq_len // GRID_W
    row = np.repeat(np.arange(rows, dtype=np.float32), GRID_W)
    col = np.tile(np.arange(GRID_W, dtype=np.float32), rows)
    axis_dim = HEAD_DIM // 2
    inv_freq = (np.float32(ROPE_BASE) ** (-np.arange(0, axis_dim, 2, dtype=np.float32) / np.float32(axis_dim)))
    ang_r = (row[:, None] * inv_freq[None, :]).astype(np.float32)
    ang_c = (col[:, None] * inv_freq[None, :]).astype(np.float32)
    cos = np.concatenate([np.cos(ang_r), np.cos(ang_c), np.cos(ang_r), np.cos(ang_c)], axis=1)
    sin = np.concatenate([-np.sin(ang_r), -np.sin(ang_c), np.sin(ang_r), np.sin(ang_c)], axis=1)
    cos = np.concatenate([cos, np.ones((tm, HEAD_DIM), np.float32)], axis=0)
    sin = np.concatenate([sin, np.zeros((tm, HEAD_DIM), np.float32)], axis=0)
    return jnp.asarray(cos, F32), jnp.asarray(sin, F32)


def _chan_dft_kernel(x_ref, g_ref, sh_ref, sc_ref, cs_ref, a_ref, b_ref, y_scr):
    _norm_mod_rows(x_ref, g_ref, sh_ref, sc_ref, y_scr, BF16)
    gw = cs_ref.shape[0]
    cs = cs_ref[...]
    for g in range(x_ref.shape[1] // gw):
        cols = slice(g * gw, (g + 1) * gw)
        r = jnp.dot(y_scr[:, cols], cs, preferred_element_type=F32)
        a_ref[:, cols] = r[:, :gw].astype(a_ref.dtype)
        b_ref[:, cols] = r[:, gw:].astype(b_ref.dtype)


def _dft_mats(n):
    k = np.arange(n, dtype=np.int64)
    ang = 2.0 * np.pi * ((k[:, None] * k[None, :]) % n).astype(np.float64) / n
    scale = 1.0 / math.sqrt(n)
    return np.cos(ang) * scale, np.sin(ang) * scale


def _chan_dft(x, gain, mod, tm):
    nt, d = x.shape
    gw = d // FNET_GROUPS
    c, s = _dft_mats(gw)
    cs = jnp.asarray(np.concatenate([c, s], axis=1), BF16)
    return pl.pallas_call(
        _chan_dft_kernel,
        grid=(nt // tm,),
        in_specs=[pl.BlockSpec((tm, d), lambda i: (i, 0)),
                  pl.BlockSpec((1, d), lambda i: (0, 0)),
                  mod.spec(0), mod.spec(1),
                  pl.BlockSpec((gw, 2 * gw), lambda i: (0, 0))],
        out_specs=[pl.BlockSpec((tm, d), lambda i: (i, 0)), pl.BlockSpec((tm, d), lambda i: (i, 0))],
        out_shape=[jax.ShapeDtypeStruct((nt, d), BF16), jax.ShapeDtypeStruct((nt, d), BF16)],
        scratch_shapes=[pltpu.VMEM((tm, d), BF16)],
        compiler_params=_params(("arbitrary",), 2 * tm * d * 4 + 5 * tm * d * 2 + (8 << 20)),
        name="fnet_channel_dft",
    )(x, gain.reshape(1, d), mod.table, mod.table, cs)


FFT_RADIX = 4


def _pos_fft_kernel(a_ref, b_ref, twc_ref, tws_ref, c_ref, s_ref, o_ref):
    q = a_ref.shape[0] // FFT_RADIX
    tn = a_ref.shape[1]
    ar = [a_ref[k * q:(k + 1) * q, :].astype(F32) for k in range(FFT_RADIX)]
    br = [b_ref[k * q:(k + 1) * q, :].astype(F32) for k in range(FFT_RADIX)]
    s02r, d02r, s13r, d13r = ar[0] + ar[2], ar[0] - ar[2], ar[1] + ar[3], ar[1] - ar[3]
    s02b, d02b, s13b, d13b = br[0] + br[2], br[0] - br[2], br[1] + br[3], br[1] - br[3]
    g = [(s02r + s13r, s02b + s13b), (d02r - d13b, d02b + d13r),
         (s02r - s13r, s02b - s13b), (d02r + d13b, d02b - d13r)]
    for j in range(FFT_RADIX):
        gr, gb = g[j]
        if j > 0:
            rows = slice(j * q, (j + 1) * q)
            tc = jnp.concatenate([twc_ref[rows, :]] * (tn // LANE), axis=1)
            ts = jnp.concatenate([tws_ref[rows, :]] * (tn // LANE), axis=1)
            gr, gb = gr * tc - gb * ts, gb * tc + gr * ts
        f = (jnp.dot(c_ref[...], gr.astype(BF16), preferred_element_type=F32)
             - jnp.dot(s_ref[...], gb.astype(BF16), preferred_element_type=F32))
        o_ref[j * q:(j + 1) * q, :] = f.astype(o_ref.dtype)


def _pos_fft(a, b, n_batch, seq_len):
    nt, d = a.shape
    q = seq_len // FFT_RADIX
    c, s = _dft_mats(q)
    scale = math.sqrt(q) / math.sqrt(seq_len)
    cq, sq = jnp.asarray(c * scale, BF16), jnp.asarray(s * scale, BF16)
    l2 = np.arange(q, dtype=np.int64)
    ang = np.concatenate([2.0 * np.pi * ((l2 * j) % seq_len) / seq_len for j in range(FFT_RADIX)])
    twc = jnp.asarray(np.broadcast_to(np.cos(ang)[:, None], (seq_len, LANE)), F32)
    tws = jnp.asarray(np.broadcast_to(np.sin(ang)[:, None], (seq_len, LANE)), F32)
    tn = _pow2_tile(256, d)
    return pl.pallas_call(
        _pos_fft_kernel,
        grid=(n_batch, d // tn),
        in_specs=[pl.BlockSpec((seq_len, tn), lambda bb, n: (bb, n)),
                  pl.BlockSpec((seq_len, tn), lambda bb, n: (bb, n)),
                  pl.BlockSpec((seq_len, LANE), lambda bb, n: (0, 0)),
                  pl.BlockSpec((seq_len, LANE), lambda bb, n: (0, 0)),
                  pl.BlockSpec((q, q), lambda bb, n: (0, 0)),
                  pl.BlockSpec((q, q), lambda bb, n: (0, 0))],
        out_specs=pl.BlockSpec((seq_len, tn), lambda bb, n: (bb, n)),
        out_shape=jax.ShapeDtypeStruct((nt, d), BF16),
        compiler_params=_params(("arbitrary", "arbitrary"),
                                6 * seq_len * tn * 2 + 4 * seq_len * LANE * 4 + 4 * q * q * 2
                                + 12 * seq_len * tn * 4 + (4 << 20)),
        name="fnet_position_fft",
    )(a, b, twc, tws, cq, sq)


def _resid_mm_interleave_kernel(f0_ref, f1_ref, f2_ref, f3_ref, p_ref, w_ref, x_ref, gt_ref, o_ref):
    fcat = jnp.concatenate([f[...] for f in (f0_ref, f1_ref, f2_ref, f3_ref)], axis=0)
    ftrue = jnp.dot(p_ref[...], fcat, preferred_element_type=F32).astype(BF16)
    o_ref[...] = x_ref[...] + gt_ref[0] * jnp.dot(ftrue, w_ref[...], preferred_element_type=F32)


def _resid_mm_interleave(f, w, x, mod, gate_chunk, seq_len, tm):
    n_rows, d = x.shape
    k = w.shape[0]
    q = seq_len // FFT_RADIX
    rj = tm // FFT_RADIX
    per_seq = seq_len // tm

    def f_map(j):
        return lambda i: ((i // per_seq) * (seq_len // rj) + j * (q // rj) + i % per_seq, 0)

    perm = np.zeros((tm, tm), np.float32)
    r = np.arange(rj)
    for j in range(FFT_RADIX):
        perm[FFT_RADIX * r + j, j * rj + r] = 1.0
    return pl.pallas_call(
        _resid_mm_interleave_kernel,
        grid=(n_rows // tm,),
        in_specs=[pl.BlockSpec((rj, k), f_map(j)) for j in range(FFT_RADIX)]
        + [pl.BlockSpec((tm, tm), lambda i: (0, 0)),
           pl.BlockSpec((k, d), lambda i: (0, 0)),
           pl.BlockSpec((tm, d), lambda i: (i, 0)),
           mod.spec(gate_chunk)],
        out_specs=pl.BlockSpec((tm, d), lambda i: (i, 0)),
        out_shape=jax.ShapeDtypeStruct((n_rows, d), F32),
        compiler_params=_params(("arbitrary",), 4 * tm * k * 2 + 2 * k * d * 2 + 6 * tm * d * 4 + (4 << 20)),
        name="fnet_out",
    )(f, f, f, f, jnp.asarray(perm, BF16), w, x, mod.table)


def kernel(x, c, ctx, c_ctx, ada_w, ada_b, norm_mix_g, norm_ffn_g, gm_w_in, gm_v_g, gm_w_s, gm_b_s, gm_w_out, cv_w_pw1, cv_b_pw1, cv_w_dw, cv_b_dw, cv_norm_g, cv_w_pw2, at_w_qkv, at_q_g, at_k_g, at_sink, at_w_o, ft_w_out, f_w_gate, f_w_up, f_w_down, m_w_router, m_b_router, m_w_gate, m_w_up, m_w_down):
    n_batch, seq_len, d = x.shape
    ctx_len = ctx.shape[1]
    depth = ada_w.shape[0]
    assert depth == 4 and seq_len % GRID_W == 0 and ctx_len % CHUNK == 0 and seq_len % CHUNK == 0
    assert d % (HEAD_DIM * KV_GROUP) == 0 and n_batch < 16
    n_lat, n_ctx = n_batch * seq_len, n_batch * ctx_len
    nt = n_lat + n_ctx
    tm = _pow2_tile(ROW_TILE, seq_len, n_ctx)
    tn = _pow2_tile(COL_TILE, d // KV_GROUP)
    tmd = _pow2_tile(DMA_ROWS, seq_len, n_ctx)
    tms = _pow2_tile(512, seq_len, n_ctx)
    bf = lambda w: w.astype(BF16)
    big = _cast_layer_bf16

    cc = jnp.zeros((16, d), F32).at[:n_batch].set(c).at[n_batch].set(c_ctx)
    table = _ada_table(cc, ada_w, ada_b).reshape(depth, 16 * 6, 1, d)
    mods = [_Mod(table[l], tm, seq_len, n_batch) for l in range(depth)]
    mods_s = [_Mod(table[l], tms, seq_len, n_batch) for l in range(depth)]
    mods_d = [_Mod(table[l], tmd, seq_len, n_batch) for l in range(depth)]

    x_lat, x_ctx = x.reshape(n_lat, d), ctx.reshape(n_ctx, d)

    h = _proj_call(functools.partial(_proj_gelu_kernel, n_lat // tms), (x_lat, x_ctx), norm_mix_g[0], mods_s[0], 0,
                   big(gm_w_in, 0), [], [], 2 * d, tms, tn, "gmlp_in")
    xs = _gmlp_tail(h, gm_v_g[0], gm_w_s[0], gm_b_s[0], bf(gm_w_out[0]), x_lat, x_ctx, mods_s[0], tms)
    xs = _ffn(xs, norm_ffn_g[0], mods[0], big(f_w_gate, 0), big(f_w_up, 0), big(f_w_down, 0), tm)

    w_pw1 = big(cv_w_pw1, 0)
    b_pw1 = cv_b_pw1[0].reshape(1, 2 * d)
    z = _proj_call(_proj_glu_kernel, xs, norm_mix_g[1], mods[1], 0, w_pw1,
                   [b_pw1], [pl.BlockSpec((1, 2 * d), lambda i: (0, 0))], d, tm, tn, "conv_in")
    tmc = _pow2_tile(256, seq_len, ctx_len)
    xs = _conv_tail(z, cv_w_dw[0], cv_b_dw[0], cv_norm_g[0], bf(cv_w_pw2[0]), xs,
                    _Mod(table[1], tmc, seq_len, n_batch), tmc, seq_len, n_lat, ctx_len)
    xs = _moe(xs, norm_ffn_g[1], mods[1], mods_d[1], m_w_router[0], m_b_router[0],
              big(m_w_gate, 0), big(m_w_up, 0), big(m_w_down, 0), tm, tmd)

    qkv_dim = at_w_qkv.shape[2]
    n_qk_tiles = (d + d // KV_GROUP) // tn
    cos, sin = _rope_tables(seq_len, tm)
    rope_block = lambda i: (jnp.where(i < n_lat // tm, i % (seq_len // tm), seq_len // tm), 0)
    q_gain = _rope_layout(at_q_g[0] * (HEAD_DIM ** -0.5 * LOG2E), 1)
    k_gain = _rope_layout(at_k_g[0], 1)
    gains = jnp.stack([q_gain] * (d // tn) + [k_gain] * (qkv_dim // tn - d // tn)).reshape(-1, 1, HEAD_DIM)
    n_qk = d + d // KV_GROUP
    w_qkv = jnp.concatenate([_rope_layout(at_w_qkv[0][:, :n_qk], n_qk // HEAD_DIM), at_w_qkv[0][:, n_qk:]], axis=1)
    qkv = _qkv_proj(xs, norm_mix_g[2], mods[2], bf(w_qkv), gains, cos, sin, rope_block, n_qk_tiles, tm, tn)
    o = _attention(qkv, at_sink[0], n_batch, seq_len, ctx_len, d)
    xl = _resid_mm(o, bf(at_w_o[0]), xs, mods_s[2], 2, n_lat, tms, "attn_out")
    xl = _ffn(xl, norm_ffn_g[2], mods[2], big(f_w_gate, 1), big(f_w_up, 1), big(f_w_down, 1), tm)

    a, b = _chan_dft(xl, norm_mix_g[3], mods[3], tm)
    f = _pos_fft(a, b, n_batch, seq_len)
    xl = _resid_mm_interleave(f, bf(ft_w_out[0]), xl, mods_s[3], 2, seq_len, tms)
    xl = _moe(xl, norm_ffn_g[3], mods[3], mods_d[3], m_w_router[1], m_b_router[1],
              big(m_w_gate, 1), big(m_w_up, 1), big(m_w_down, 1), tm, tmd)
    return xl.reshape(n_batch, seq_len, d)
```
